```python
import jax, jax.numpy as jnp
from jax import lax
import numpy as np

D_MODEL = 2048
BATCH = 1
SEQ = 8192
DEPTH = 1

EPS = 1e-6
MEM_LEN = 256

CHUNK = 128
A_GROUP_DIM = 128
A_GROUPS = D_MODEL // A_GROUP_DIM
A_WIDTH = A_GROUPS * A_GROUP_DIM

QK_NOPE = 128
QK_ROPE = 64
V_DIM = 128
MLA_HEADS = D_MODEL // V_DIM
Q_LORA = 512
KV_LORA = 512
MLA_WIDTH = MLA_HEADS * V_DIM
QK_DIM = QK_NOPE + QK_ROPE
ROPE_THETA = 10000.0
Q_BLOCK = 128

MEM_HEADS = 4
MEM_HEAD_DIM = D_MODEL // MEM_HEADS
MEM_WIDTH = MEM_HEADS * MEM_HEAD_DIM

N_BRANCH = 3
BRANCH_WIDTH = D_MODEL

IN_SIZES = (A_WIDTH, A_WIDTH, A_WIDTH,
            Q_LORA, KV_LORA, QK_ROPE, MLA_WIDTH,
            MEM_WIDTH, MEM_WIDTH)
IN_TOTAL = int(sum(IN_SIZES))
IN_SPLITS = [int(o) for o in np.cumsum(IN_SIZES)[:-1]]

kernel_name = "hybrid_gmlp_mla_memory_gated"


def rmsnorm(x, g):
    xf = x.astype(jnp.float32)
    xf = xf * lax.rsqrt(jnp.mean(xf * xf, axis=-1, keepdims=True) + EPS)
    return xf.astype(x.dtype) * g


def layernorm(x, g, b):
    xf = x.astype(jnp.float32)
    mu = jnp.mean(xf, axis=-1, keepdims=True)
    var = jnp.mean(jnp.square(xf - mu), axis=-1, keepdims=True)
    return ((xf - mu) * lax.rsqrt(var + EPS)).astype(x.dtype) * g + b


def rope_tables(positions):
    inv_freq = 1.0 / (ROPE_THETA ** (jnp.arange(0, QK_ROPE, 2, dtype=jnp.float32) / QK_ROPE))
    ang = positions.astype(jnp.float32)[..., None] * inv_freq
    return jnp.cos(ang), jnp.sin(ang)


def apply_rope(t, cos, sin):
    t1, t2 = jnp.split(t, 2, axis=-1)
    cos = cos.astype(t.dtype)
    sin = sin.astype(t.dtype)
    return jnp.concatenate([t1 * cos - t2 * sin, t2 * cos + t1 * sin], axis=-1)


def chunked_spatial_gating(u_raw, v_raw, ln_g, ln_b, w_s, b_s):
    B, S, _ = u_raw.shape
    u = jax.nn.gelu(u_raw)
    v = layernorm(jax.nn.gelu(v_raw), ln_g, ln_b)
    vc = v.reshape(B, S // CHUNK, CHUNK, A_GROUPS, A_GROUP_DIM)
    causal = jnp.tril(jnp.ones((CHUNK, CHUNK), dtype=w_s.dtype))
    ws = w_s * causal[None]
    sv = jnp.einsum('gts,bcsgd->bctgd', ws, vc) + b_s.T[None, None, :, :, None]
    return u * sv.reshape(B, S, A_WIDTH)


def latent_attention(c_q, c_kv, k_rope, cos, sin, q_norm_g, w_uq, kv_norm_g, w_ukv):
    B, S, _ = c_q.shape
    q = (rmsnorm(c_q, q_norm_g) @ w_uq).reshape(B, S, MLA_HEADS, QK_DIM)
    q_nope, q_pe = jnp.split(q, [QK_NOPE], axis=-1)
    q_pe = apply_rope(q_pe, cos[:, :, None, :], sin[:, :, None, :])
    q = jnp.concatenate([q_nope, q_pe], axis=-1)

    kv = (rmsnorm(c_kv, kv_norm_g) @ w_ukv).reshape(B, S, MLA_HEADS, QK_NOPE + V_DIM)
    k_nope, v = jnp.split(kv, [QK_NOPE], axis=-1)
    k_pe = apply_rope(k_rope, cos, sin)
    k_pe = jnp.broadcast_to(k_pe[:, :, None, :], (B, S, MLA_HEADS, QK_ROPE))
    k = jnp.concatenate([k_nope, k_pe], axis=-1)

    qh = q.transpose(0, 2, 1, 3)
    kh = k.transpose(0, 2, 1, 3)
    vh = v.transpose(0, 2, 1, 3)
    n_blk = S // Q_BLOCK
    q_blocks = qh.reshape(B, MLA_HEADS, n_blk, Q_BLOCK, QK_DIM).transpose(2, 0, 1, 3, 4)
    scale = QK_DIM ** -0.5
    key_pos = jnp.arange(S)

    def one_block(args):
        qi, bi = args
        s = jnp.einsum('bhqd,bhkd->bhqk', qi, kh).astype(jnp.float32) * scale
        q_pos = bi * Q_BLOCK + jnp.arange(Q_BLOCK)
        mask = key_pos[None, :] <= q_pos[:, None]
        s = jnp.where(mask[None, None], s, -1e30)
        p = jax.nn.softmax(s, axis=-1).astype(vh.dtype)
        return jnp.einsum('bhqk,bhkd->bhqd', p, vh)

    o = lax.map(one_block, (q_blocks, jnp.arange(n_blk)))
    return o.transpose(1, 0, 3, 2, 4).reshape(B, S, MLA_WIDTH)


def memory_attention(q_m, mem, mem_norm_g, w_mem_kv):
    B, S, _ = q_m.shape
    kv = rmsnorm(mem, mem_norm_g) @ w_mem_kv
    k_m, v_m = jnp.split(kv.reshape(B, MEM_LEN, 2, MEM_HEADS, MEM_HEAD_DIM), 2, axis=2)
    k_m, v_m = k_m[:, :, 0], v_m[:, :, 0]
    q = q_m.reshape(B, S, MEM_HEADS, MEM_HEAD_DIM)
    s = jnp.einsum('bshd,bmhd->bhsm', q, k_m).astype(jnp.float32) * (MEM_HEAD_DIM ** -0.5)
    p = jax.nn.softmax(s, axis=-1).astype(v_m.dtype)
    return jnp.einsum('bhsm,bmhd->bshd', p, v_m).reshape(B, S, MEM_WIDTH)


def hybrid_layer(x, mem, cos, sin, g_pre, w_in, a_ln_g, a_ln_b, a_w_s, a_b_s,
                 q_norm_g, w_uq, kv_norm_g, w_ukv, mem_norm_g, w_mem_kv,
                 w_gate, b_gate, w_branch, w_out, g_post):
    B, S, D = x.shape
    h = rmsnorm(x, g_pre)
    proj = h @ w_in
    u, v, z_a, c_q, c_kv, k_rope, z_b, q_m, z_m = jnp.split(proj, IN_SPLITS, axis=-1)

    y_a = chunked_spatial_gating(u, v, a_ln_g, a_ln_b, a_w_s, a_b_s) * jax.nn.silu(z_a)
    y_b = latent_attention(c_q, c_kv, k_rope, cos, sin, q_norm_g, w_uq, kv_norm_g, w_ukv) * jax.nn.silu(z_b)
    y_m = memory_attention(q_m, mem, mem_norm_g, w_mem_kv) * jax.nn.silu(z_m)

    y = jnp.stack([y_a, y_b, y_m], axis=2)
    p = jnp.einsum('bsnc,ncd->bsnd', y, w_branch)
    gates = jax.nn.sigmoid(h @ w_gate + b_gate).reshape(B, S, N_BRANCH, D)
    merged = jnp.sum(gates * p, axis=2)
    out = merged @ w_out
    return x + rmsnorm(out, g_post)


def setup_inputs(seed: int = 0) -> dict:
    key = jax.random.key(seed)
    ks = jax.random.split(key, 24)
    f32 = jnp.float32
    L, D = DEPTH, D_MODEL

    def w(k, shape, fan_in):
        return jax.random.normal(k, shape, f32) * (fan_in ** -0.5)

    def gain(k, shape):
        return 1.0 + 0.02 * jax.random.normal(k, shape, f32)

    def bias(k, shape):
        return 0.01 * jax.random.normal(k, shape, f32)

    return {
        "x": jax.random.normal(ks[0], (BATCH, SEQ, D), f32),
        "mem": jax.random.normal(ks[1], (BATCH, MEM_LEN, D), f32),
        "positions": jnp.broadcast_to(jnp.arange(SEQ, dtype=jnp.int32)[None], (BATCH, SEQ)),
        "g_pre": gain(ks[2], (L, D)),
        "w_in": w(ks[3], (L, D, IN_TOTAL), D),
        "a_ln_g": gain(ks[4], (L, A_WIDTH)),
        "a_ln_b": bias(ks[5], (L, A_WIDTH)),
        "a_w_s": w(ks[6], (L, A_GROUPS, CHUNK, CHUNK), CHUNK),
        "a_b_s": gain(ks[7], (L, A_GROUPS, CHUNK)),
        "q_norm_g": gain(ks[8], (L, Q_LORA)),
        "w_uq": w(ks[9], (L, Q_LORA, MLA_HEADS * QK_DIM), Q_LORA),
        "kv_norm_g": gain(ks[10], (L, KV_LORA)),
        "w_ukv": w(ks[11], (L, KV_LORA, MLA_HEADS * (QK_NOPE + V_DIM)), KV_LORA),
        "mem_norm_g": gain(ks[12], (L, D)),
        "w_mem_kv": w(ks[13], (L, D, 2 * MEM_WIDTH), D),
        "w_gate": w(ks[14], (L, D, N_BRANCH * D), D),
        "b_gate": bias(ks[15], (L, N_BRANCH * D)),
        "w_branch": w(ks[16], (L, N_BRANCH, BRANCH_WIDTH, D), BRANCH_WIDTH),
        "w_out": w(ks[17], (L, D, D), D),
        "g_post": gain(ks[18], (L, D)),
    }


def reference(x, mem, positions, g_pre, w_in, a_ln_g, a_ln_b, a_w_s, a_b_s,
              q_norm_g, w_uq, kv_norm_g, w_ukv, mem_norm_g, w_mem_kv,
              w_gate, b_gate, w_branch, w_out, g_post):
    cos, sin = rope_tables(positions)
    for l in range(DEPTH):
        x = hybrid_layer(x, mem, cos, sin, g_pre[l], w_in[l], a_ln_g[l], a_ln_b[l],
                         a_w_s[l], a_b_s[l], q_norm_g[l], w_uq[l], kv_norm_g[l], w_ukv[l],
                         mem_norm_g[l], w_mem_kv[l], w_gate[l], b_gate[l], w_branch[l],
                         w_out[l], g_post[l])
    return x
```

```python
import functools

import jax
import jax.numpy as jnp
from jax import lax
from jax.experimental import pallas as pl
from jax.experimental.pallas import tpu as pltpu

EPS = 1e-6
CHUNK = 128
A_GROUPS = 16
QK_NOPE = 128
QK_ROPE = 64
V_DIM = 128
MLA_HEADS = 16
Q_LORA = 512
KV_LORA = 512
QK_DIM = QK_NOPE + QK_ROPE
ROPE_THETA = 10000.0
MEM_HEADS = 4
N_BRANCH = 3
LANE = 128
HEAD_PAD = 256
VMEM_LIMIT = 52 * 1024 * 1024

F32 = jnp.float32
BF16 = jnp.bfloat16


def _cparams(*sem):
    return pltpu.CompilerParams(dimension_semantics=sem, vmem_limit_bytes=VMEM_LIMIT)


def _dot(a, b):
    return jnp.dot(a, b, preferred_element_type=F32)


def _dot_nt(a, b):
    return lax.dot_general(a, b, (((1,), (1,)), ((), ())), preferred_element_type=F32)


def _gelu(x):
    return 0.5 * x * (1.0 + jnp.tanh(0.7978845608028654 * (x + 0.044715 * (x * x * x))))


def _sigmoid(x):
    return 1.0 / (1.0 + jnp.exp(-x))


def _silu(x):
    return x * _sigmoid(x)


def _rms(xf, g):
    return xf * lax.rsqrt(jnp.mean(xf * xf, axis=-1, keepdims=True) + EPS) * g


def _rope_table_kernel(pos_ref, invf_ref, cos_ref, sin_ref):
    ang = pos_ref[...].astype(F32) * invf_ref[...]
    cos_ref[...] = jnp.cos(ang)
    sin_ref[...] = jnp.sin(ang)


def _rope_tables(positions, seq):
    half = QK_ROPE // 2
    inv_freq = 1.0 / (ROPE_THETA ** (jnp.arange(0, QK_ROPE, 2, dtype=F32) / QK_ROPE))
    cos_t, sin_t = pl.pallas_call(
        _rope_table_kernel,
        out_shape=(jax.ShapeDtypeStruct((half, seq), F32),) * 2,
        name="rope_tables",
    )(positions.reshape(1, seq), inv_freq.reshape(half, 1))
    cos, sin = cos_t.T, sin_t.T
    z32 = jnp.zeros((seq, half), F32)
    z64 = jnp.zeros((seq, 2 * half), F32)
    tab_c = jnp.concatenate([cos, cos, z64], axis=-1)
    tab_a = jnp.concatenate([sin, z32, z64], axis=-1)
    tab_b = jnp.concatenate([z32, sin, z64], axis=-1)
    return tab_c, tab_a, tab_b


def _rope(t, tab_c, tab_a, tab_b):
    down = pltpu.roll(t, 3 * LANE // 4, axis=1)
    up = pltpu.roll(t, LANE // 4, axis=1)
    return t * tab_c - down * tab_a + up * tab_b


def _proj_kernel(x_ref, g_ref, w_ref, b_ref, o_ref, h_ref, *, bounds):
    j = pl.program_id(1)

    @pl.when(j == 0)
    def _():
        h_ref[...] = _rms(x_ref[...], g_ref[...]).astype(BF16)

    acts = (_gelu, _silu, _sigmoid, lambda v: v)
    lo = 0
    for hi, act in zip(bounds, acts):
        @pl.when((j >= lo) & (j < hi))
        def _(act=act):
            acc = _dot(h_ref[...], w_ref[...]) + b_ref[...]
            o_ref[...] = act(acc).astype(o_ref.dtype)
        lo = hi


def _projection(x2, g_pre, w_all, b_all, bounds, tm, tn):
    seq, d = x2.shape
    n = w_all.shape[1]
    return pl.pallas_call(
        functools.partial(_proj_kernel, bounds=bounds),
        grid=(seq // tm, n // tn),
        in_specs=[
            pl.BlockSpec((tm, d), lambda i, j: (i, 0)),
            pl.BlockSpec((1, d), lambda i, j: (0, 0)),
            pl.BlockSpec((d, tn), lambda i, j: (0, j)),
            pl.BlockSpec((1, tn), lambda i, j: (0, j)),
        ],
        out_specs=pl.BlockSpec((tm, tn), lambda i, j: (i, j)),
        out_shape=jax.ShapeDtypeStruct((seq, n), BF16),
        scratch_shapes=[pltpu.VMEM((tm, d), BF16)],
        compiler_params=_cparams("parallel", "arbitrary"),
        name="in_proj",
    )(x2, g_pre, w_all, b_all)


def _branch_a_kernel(gu_ref, gv_ref, sz_ref, lng_ref, lnb_ref, ws_ref, bs_ref, o_ref, *, tm):
    gv = gv_ref[...].astype(F32)
    mu = jnp.mean(gv, axis=-1, keepdims=True)
    cen = gv - mu
    var = jnp.mean(cen * cen, axis=-1, keepdims=True)
    vn = (cen * lax.rsqrt(var + EPS) * lng_ref[...] + lnb_ref[...]).astype(BF16)
    row = lax.broadcasted_iota(jnp.int32, (CHUNK, CHUNK), 0)
    col = lax.broadcasted_iota(jnp.int32, (CHUNK, CHUNK), 1)
    causal = col <= row
    for g in range(A_GROUPS):
        ws = jnp.where(causal, ws_ref[g], 0.0).astype(BF16)
        cs = slice(g * CHUNK, (g + 1) * CHUNK)
        for c in range(tm // CHUNK):
            rs = slice(c * CHUNK, (c + 1) * CHUNK)
            sv = _dot(ws, vn[rs, cs]) + bs_ref[g]
            y = gu_ref[rs, cs].astype(F32) * sv * sz_ref[rs, cs].astype(F32)
            o_ref[rs, cs] = y.astype(o_ref.dtype)


def _branch_a(proj, ln_g, ln_b, w_s, b_s_b, offs, tm):
    seq = proj.shape[0]
    w = A_GROUPS * CHUNK
    blk = lambda off: pl.BlockSpec((tm, w), lambda i, off=off: (i, off // w))
    const = lambda shape: pl.BlockSpec(shape, lambda i: (0,) * len(shape))
    return pl.pallas_call(
        functools.partial(_branch_a_kernel, tm=tm),
        grid=(seq // tm,),
        in_specs=[blk(offs["u"]), blk(offs["v"]), blk(offs["za"]),
                  const((1, w)), const((1, w)),
                  const((A_GROUPS, CHUNK, CHUNK)), const((A_GROUPS, CHUNK, CHUNK))],
        out_specs=pl.BlockSpec((tm, w), lambda i: (i, 0)),
        out_shape=jax.ShapeDtypeStruct((seq, w), BF16),
        compiler_params=_cparams("parallel"),
        name="branch_a",
    )(proj, proj, proj, ln_g, ln_b, w_s, b_s_b)


def _mla_prep_kernel(cq_ref, ckv_ref, kr_ref, tc_ref, ta_ref, tb_ref, qg_ref, kvg_ref,
                     wq_ref, wkv_ref, q_ref, k_ref, v_ref):
    scale = QK_DIM ** -0.5
    tab_c, tab_a, tab_b = tc_ref[...], ta_ref[...], tb_ref[...]
    cqn = _rms(cq_ref[...].astype(F32), qg_ref[...]).astype(BF16)
    ckvn = _rms(ckv_ref[...].astype(F32), kvg_ref[...]).astype(BF16)
    kpe = _rope(kr_ref[...].astype(F32), tab_c, tab_a, tab_b).astype(BF16)
    for h in range(MLA_HEADS):
        lo, mid, hi = h * HEAD_PAD, h * HEAD_PAD + LANE, (h + 1) * HEAD_PAD
        q = _dot(cqn, wq_ref[:, lo:hi]) * scale
        q_ref[:, lo:mid] = q[:, :LANE].astype(BF16)
        q_ref[:, mid:hi] = _rope(q[:, LANE:], tab_c, tab_a, tab_b).astype(BF16)
        kv = _dot(ckvn, wkv_ref[:, lo:hi])
        k_ref[:, lo:mid] = kv[:, :LANE].astype(BF16)
        k_ref[:, mid:hi] = kpe
        v_ref[:, h * V_DIM:(h + 1) * V_DIM] = kv[:, LANE:].astype(BF16)


def _mla_prep(proj, tabs, q_norm_g, kv_norm_g, wq_pad, wkv, offs, tm):
    seq = proj.shape[0]
    hw = MLA_HEADS * HEAD_PAD
    const = lambda shape: pl.BlockSpec(shape, lambda i: (0,) * len(shape))
    col = lambda off, w: pl.BlockSpec((tm, w), lambda i: (i, off // w))
    tab = pl.BlockSpec((tm, LANE), lambda i: (i, 0))
    return pl.pallas_call(
        _mla_prep_kernel,
        grid=(seq // tm,),
        in_specs=[col(offs["cq"], Q_LORA), col(offs["ckv"], KV_LORA), col(offs["kr"], LANE),
                  tab, tab, tab, const((1, Q_LORA)), const((1, KV_LORA)),
                  const((Q_LORA, hw)), const((KV_LORA, hw))],
        out_specs=[pl.BlockSpec((tm, hw), lambda i: (i, 0)),
                   pl.BlockSpec((tm, hw), lambda i: (i, 0)),
                   pl.BlockSpec((tm, MLA_HEADS * V_DIM), lambda i: (i, 0))],
        out_shape=[jax.ShapeDtypeStruct((seq, hw), BF16),
                   jax.ShapeDtypeStruct((seq, hw), BF16),
                   jax.ShapeDtypeStruct((seq, MLA_HEADS * V_DIM), BF16)],
        compiler_params=_cparams("parallel"),
        name="mla_prep",
    )(proj, proj, proj, *tabs, q_norm_g, kv_norm_g, wq_pad, wkv)


def _flash_kernel(q_ref, k_ref, v_ref, g_ref, o_ref, m_ref, l_ref, acc_ref, *, t):
    qi = pl.program_id(1)
    q = q_ref[...]
    m_ref[...] = jnp.full(m_ref.shape, -1e30, F32)
    l_ref[...] = jnp.zeros(l_ref.shape, F32)
    acc_ref[...] = jnp.zeros(acc_ref.shape, F32)

    def step(j, masked):
        start = pl.multiple_of(j * t, t)
        s = _dot_nt(q, k_ref[pl.ds(start, t), :])
        if masked:
            row = lax.broadcasted_iota(jnp.int32, (t, t), 0)
            col = lax.broadcasted_iota(jnp.int32, (t, t), 1)
            s = jnp.where(col <= row, s, -1e30)
        m_prev = m_ref[...]
        m_new = jnp.maximum(m_prev, jnp.max(s, axis=-1, keepdims=True))
        alpha = jnp.exp(m_prev - m_new)
        p = jnp.exp(s - m_new)
        l_ref[...] = alpha * l_ref[...] + jnp.sum(p, axis=-1, keepdims=True)
        acc_ref[...] = alpha * acc_ref[...] + _dot(p.astype(BF16), v_ref[pl.ds(start, t), :])
        m_ref[...] = m_new

    def body(j, carry):
        step(j, False)
        return carry

    lax.fori_loop(0, qi, body, 0)
    step(qi, True)
    o_ref[...] = (acc_ref[...] / l_ref[...] * g_ref[...].astype(F32)).astype(o_ref.dtype)


def _flash(q, k, v, proj, zb_off, t):
    seq = q.shape[0]
    return pl.pallas_call(
        functools.partial(_flash_kernel, t=t),
        grid=(MLA_HEADS, seq // t),
        in_specs=[pl.BlockSpec((t, HEAD_PAD), lambda h, i: (i, h)),
                  pl.BlockSpec((seq, HEAD_PAD), lambda h, i: (0, h)),
                  pl.BlockSpec((seq, V_DIM), lambda h, i: (0, h)),
                  pl.BlockSpec((t, V_DIM), lambda h, i: (i, zb_off // V_DIM + h))],
        out_specs=pl.BlockSpec((t, V_DIM), lambda h, i: (i, h)),
        out_shape=jax.ShapeDtypeStruct((seq, MLA_HEADS * V_DIM), BF16),
        scratch_shapes=[pltpu.VMEM((t, 1), F32), pltpu.VMEM((t, 1), F32),
                        pltpu.VMEM((t, V_DIM), F32)],
        compiler_params=_cparams("parallel", "arbitrary"),
        name="flash_mla",
    )(q, k, v, proj)


def _mem_kv_kernel(mem_ref, g_ref, w_ref, o_ref):
    mn = _rms(mem_ref[...], g_ref[...]).astype(BF16)
    o_ref[...] = _dot(mn, w_ref[...]).astype(o_ref.dtype)


def _mem_kv(mem2, mem_norm_g, w_mem_kv, tn):
    m, d = mem2.shape
    n = w_mem_kv.shape[1]
    return pl.pallas_call(
        _mem_kv_kernel,
        grid=(n // tn,),
        in_specs=[pl.BlockSpec((m, d), lambda j: (0, 0)),
                  pl.BlockSpec((1, d), lambda j: (0, 0)),
                  pl.BlockSpec((d, tn), lambda j: (0, j))],
        out_specs=pl.BlockSpec((m, tn), lambda j: (0, j)),
        out_shape=jax.ShapeDtypeStruct((m, n), BF16),
        compiler_params=_cparams("parallel"),
        name="mem_kv",
    )(mem2, mem_norm_g, w_mem_kv)


def _mem_attn_kernel(q_ref, kv_ref, sz_ref, o_ref, *, width):
    hd = width // MEM_HEADS
    scale = hd ** -0.5
    for h in range(MEM_HEADS):
        cs = slice(h * hd, (h + 1) * hd)
        s = _dot_nt(q_ref[:, cs], kv_ref[:, cs]) * scale
        p = jnp.exp(s - jnp.max(s, axis=-1, keepdims=True))
        l = jnp.sum(p, axis=-1, keepdims=True)
        o = _dot(p.astype(BF16), kv_ref[:, width + h * hd:width + (h + 1) * hd]) / l
        o_ref[:, cs] = (o * sz_ref[:, cs].astype(F32)).astype(o_ref.dtype)


def _mem_attn(proj, kvm, offs, width, tm):
    seq = proj.shape[0]
    blk = lambda off: pl.BlockSpec((tm, width), lambda i: (i, off // width))
    return pl.pallas_call(
        functools.partial(_mem_attn_kernel, width=width),
        grid=(seq // tm,),
        in_specs=[blk(offs["qm"]), pl.BlockSpec(kvm.shape, lambda i: (0, 0)), blk(offs["zm"])],
        out_specs=pl.BlockSpec((tm, width), lambda i: (i, 0)),
        out_shape=jax.ShapeDtypeStruct((seq, width), BF16),
        compiler_params=_cparams("parallel"),
        name="mem_attn",
    )(proj, kvm, proj)


def _merge_kernel(ya_ref, yb_ref, ym_ref, w_ref, g0_ref, g1_ref, g2_ref, o_ref):
    acc = g0_ref[...].astype(F32) * _dot(ya_ref[...], w_ref[0])
    acc += g1_ref[...].astype(F32) * _dot(yb_ref[...], w_ref[1])
    acc += g2_ref[...].astype(F32) * _dot(ym_ref[...], w_ref[2])
    o_ref[...] = acc.astype(o_ref.dtype)


def _merge(y_a, y_b, y_m, w_branch, proj, gate_off, tm, tn):
    seq, width = y_a.shape
    d = w_branch.shape[2]
    yblk = pl.BlockSpec((tm, width), lambda i, j: (i, 0))
    gblk = lambda n: pl.BlockSpec((tm, tn), lambda i, j, n=n: (i, (gate_off + n * d) // tn + j))
    return pl.pallas_call(
        _merge_kernel,
        grid=(seq // tm, d // tn),
        in_specs=[yblk, yblk, yblk,
                  pl.BlockSpec((N_BRANCH, width, tn), lambda i, j: (0, 0, j)),
                  gblk(0), gblk(1), gblk(2)],
        out_specs=pl.BlockSpec((tm, tn), lambda i, j: (i, j)),
        out_shape=jax.ShapeDtypeStruct((seq, d), BF16),
        compiler_params=_cparams("parallel", "arbitrary"),
        name="merge",
    )(y_a, y_b, y_m, w_branch, proj, proj, proj)


def _out_kernel(m_ref, w_ref, x_ref, g_ref, o_ref):
    out = _dot(m_ref[...], w_ref[...])
    o_ref[...] = x_ref[...] + _rms(out, g_ref[...])


def _out_proj(merged, w_out, x2, g_post, tm):
    seq, d = x2.shape
    return pl.pallas_call(
        _out_kernel,
        grid=(seq // tm,),
        in_specs=[pl.BlockSpec((tm, d), lambda i: (i, 0)),
                  pl.BlockSpec((d, d), lambda i: (0, 0)),
                  pl.BlockSpec((tm, d), lambda i: (i, 0)),
                  pl.BlockSpec((1, d), lambda i: (0, 0))],
        out_specs=pl.BlockSpec((tm, d), lambda i: (i, 0)),
        out_shape=jax.ShapeDtypeStruct((seq, d), F32),
        compiler_params=_cparams("parallel"),
        name="out_proj",
    )(merged, w_out, x2, g_post)


def _layer(x2, mem2, tabs, g_pre, w_in, a_ln_g, a_ln_b, a_w_s, a_b_s, q_norm_g, w_uq,
           kv_norm_g, w_ukv, mem_norm_g, w_mem_kv, w_gate, b_gate, w_branch, w_out, g_post):
    seq, d = x2.shape
    tn = 512
    sizes = (d, d, d, Q_LORA, KV_LORA, QK_ROPE, d, d)
    u, v, za, cq, ckv, kr, zb, qm, zm = jnp.split(
        w_in, [sum(sizes[:n + 1]) for n in range(len(sizes))], axis=1)
    kr_pad = jnp.zeros((d, tn - QK_ROPE), w_in.dtype)
    w_all = jnp.concatenate([u, v, za, zb, zm, w_gate, qm, cq, ckv, kr, kr_pad], axis=1).astype(BF16)
    offs = {"u": 0, "v": d, "za": 2 * d, "zb": 3 * d, "zm": 4 * d, "gate": 5 * d,
            "qm": 8 * d, "cq": 9 * d, "ckv": 9 * d + Q_LORA, "kr": 9 * d + Q_LORA + KV_LORA}
    n_all = w_all.shape[1]
    b_all = jnp.zeros((1, n_all), F32).at[0, offs["gate"]:offs["gate"] + N_BRANCH * d].set(b_gate)
    bounds = (offs["za"] // tn, offs["gate"] // tn, offs["qm"] // tn, n_all // tn)

    proj = _projection(x2, g_pre.reshape(1, d), w_all, b_all, bounds, tm=1024, tn=tn)

    b_s_b = jnp.broadcast_to(a_b_s[:, :, None], (A_GROUPS, CHUNK, CHUNK))
    y_a = _branch_a(proj, a_ln_g.reshape(1, d), a_ln_b.reshape(1, d), a_w_s, b_s_b, offs, tm=256)

    wq_pad = jnp.pad(w_uq.reshape(Q_LORA, MLA_HEADS, QK_DIM),
                     ((0, 0), (0, 0), (0, HEAD_PAD - QK_DIM))).reshape(Q_LORA, -1).astype(BF16)
    q, k, vv = _mla_prep(proj, tabs, q_norm_g.reshape(1, -1), kv_norm_g.reshape(1, -1),
                         wq_pad, w_ukv.astype(BF16), offs, tm=512)
    y_b = _flash(q, k, vv, proj, offs["zb"], t=512)

    kvm = _mem_kv(mem2, mem_norm_g.reshape(1, d), w_mem_kv.astype(BF16), tn=1024)
    y_m = _mem_attn(proj, kvm, offs, d, tm=512)

    merged = _merge(y_a, y_b, y_m, w_branch.astype(BF16), proj, offs["gate"], tm=512, tn=512)
    return _out_proj(merged, w_out.astype(BF16), x2, g_post.reshape(1, d), tm=512)


def kernel(x, mem, positions, g_pre, w_in, a_ln_g, a_ln_b, a_w_s, a_b_s, q_norm_g, w_uq,
           kv_norm_g, w_ukv, mem_norm_g, w_mem_kv, w_gate, b_gate, w_branch, w_out, g_post):
    batch, seq, d = x.shape
    depth = g_pre.shape[0]
    outs = []
    for b in range(batch):
        tabs = _rope_tables(positions[b], seq)
        xb = x[b]
        for l in range(depth):
            xb = _layer(xb, mem[b], tabs, g_pre[l], w_in[l], a_ln_g[l], a_ln_b[l], a_w_s[l],
                        a_b_s[l], q_norm_g[l], w_uq[l], kv_norm_g[l], w_ukv[l], mem_norm_g[l],
                        w_mem_kv[l], w_gate[l], b_gate[l], w_branch[l], w_out[l], g_post[l])
        outs.append(xb)
    return jnp.stack(outs, axis=0)
```

```python
import functools

import jax
import jax.numpy as jnp
from jax import lax
from jax.experimental import pallas as pl
from jax.experimental.pallas import tpu as pltpu

EPS = 1e-6
CHUNK = 128
A_GROUPS = 16
QK_NOPE = 128
QK_ROPE = 64
V_DIM = 128
MLA_HEADS = 16
Q_LORA = 512
KV_LORA = 512
QK_DIM = QK_NOPE + QK_ROPE
ROPE_THETA = 10000.0
LOG2E = 1.4426950408889634
MEM_HEADS = 4
N_BRANCH = 3
LANE = 128
HEAD_PAD = 256
VMEM_LIMIT = 52 * 1024 * 1024

F32 = jnp.float32
BF16 = jnp.bfloat16


def _cparams(*sem):
    return pltpu.CompilerParams(dimension_semantics=sem, vmem_limit_bytes=VMEM_LIMIT)


def _dot(a, b):
    return jnp.dot(a, b, preferred_element_type=F32)


def _dot_nt(a, b):
    return lax.dot_general(a, b, (((1,), (1,)), ((), ())), preferred_element_type=F32)


def _gelu(x):
    return 0.5 * x * (1.0 + jnp.tanh(0.7978845608028654 * (x + 0.044715 * (x * x * x))))


def _sigmoid(x):
    return 1.0 / (1.0 + jnp.exp(-x))


def _silu(x):
    return x * _sigmoid(x)


def _rms(xf, g):
    return xf * lax.rsqrt(jnp.mean(xf * xf, axis=-1, keepdims=True) + EPS) * g


def _rope_table_kernel(pos_ref, invf_ref, cos_ref, sin_ref):
    ang = pos_ref[...].astype(F32) * invf_ref[...]
    cos_ref[...] = jnp.cos(ang)
    sin_ref[...] = jnp.sin(ang)


def _rope_tables(positions, seq):
    half = QK_ROPE // 2
    inv_freq = 1.0 / (ROPE_THETA ** (jnp.arange(0, QK_ROPE, 2, dtype=F32) / QK_ROPE))
    cos_t, sin_t = pl.pallas_call(
        _rope_table_kernel,
        out_shape=(jax.ShapeDtypeStruct((half, seq), F32),) * 2,
        name="rope_tables",
    )(positions.reshape(1, seq), inv_freq.reshape(half, 1))
    cos, sin = cos_t.T, sin_t.T
    z32 = jnp.zeros((seq, half), F32)
    z64 = jnp.zeros((seq, 2 * half), F32)
    tab_c = jnp.concatenate([cos, cos, z64], axis=-1)
    tab_a = jnp.concatenate([sin, z32, z64], axis=-1)
    tab_b = jnp.concatenate([z32, sin, z64], axis=-1)
    return tab_c, tab_a, tab_b


def _rope(t, tab_c, tab_a, tab_b):
    down = pltpu.roll(t, 3 * LANE // 4, axis=1)
    up = pltpu.roll(t, LANE // 4, axis=1)
    return t * tab_c - down * tab_a + up * tab_b


def _proj_kernel(x_ref, g_ref, w_ref, b_ref, o_ref, h_ref, *, bounds):
    j = pl.program_id(1)

    @pl.when(j == 0)
    def _():
        h_ref[...] = _rms(x_ref[...], g_ref[...]).astype(BF16)

    acts = (_gelu, _silu, _sigmoid, lambda v: v)
    lo = 0
    for hi, act in zip(bounds, acts):
        @pl.when((j >= lo) & (j < hi))
        def _(act=act):
            acc = _dot(h_ref[...], w_ref[...]) + b_ref[...]
            o_ref[...] = act(acc).astype(o_ref.dtype)
        lo = hi


def _projection(x2, g_pre, w_all, b_all, bounds, tm, tn):
    seq, d = x2.shape
    n = w_all.shape[1]
    return pl.pallas_call(
        functools.partial(_proj_kernel, bounds=bounds),
        grid=(seq // tm, n // tn),
        in_specs=[
            pl.BlockSpec((tm, d), lambda i, j: (i, 0)),
            pl.BlockSpec((1, d), lambda i, j: (0, 0)),
            pl.BlockSpec((d, tn), lambda i, j: (0, j)),
            pl.BlockSpec((1, tn), lambda i, j: (0, j)),
        ],
        out_specs=pl.BlockSpec((tm, tn), lambda i, j: (i, j)),
        out_shape=jax.ShapeDtypeStruct((seq, n), BF16),
        scratch_shapes=[pltpu.VMEM((tm, d), BF16)],
        compiler_params=_cparams("parallel", "arbitrary"),
        name="in_proj",
    )(x2, g_pre, w_all, b_all)


def _branch_a_kernel(gu_ref, gv_ref, sz_ref, lng_ref, lnb_ref, ws_ref, bs_ref, o_ref, *, tm):
    gv = gv_ref[...].astype(F32)
    mu = jnp.mean(gv, axis=-1, keepdims=True)
    cen = gv - mu
    var = jnp.mean(cen * cen, axis=-1, keepdims=True)
    vn = (cen * lax.rsqrt(var + EPS) * lng_ref[...] + lnb_ref[...]).astype(BF16)
    row = lax.broadcasted_iota(jnp.int32, (CHUNK, CHUNK), 0)
    col = lax.broadcasted_iota(jnp.int32, (CHUNK, CHUNK), 1)
    causal = col <= row
    for g in range(A_GROUPS):
        ws = jnp.where(causal, ws_ref[g], 0.0).astype(BF16)
        cs = slice(g * CHUNK, (g + 1) * CHUNK)
        for c in range(tm // CHUNK):
            rs = slice(c * CHUNK, (c + 1) * CHUNK)
            sv = _dot(ws, vn[rs, cs]) + bs_ref[g]
            y = gu_ref[rs, cs].astype(F32) * sv * sz_ref[rs, cs].astype(F32)
            o_ref[rs, cs] = y.astype(o_ref.dtype)


def _branch_a(proj, ln_g, ln_b, w_s, b_s_b, offs, tm):
    seq = proj.shape[0]
    w = A_GROUPS * CHUNK
    blk = lambda off: pl.BlockSpec((tm, w), lambda i, off=off: (i, off // w))
    const = lambda shape: pl.BlockSpec(shape, lambda i: (0,) * len(shape))
    return pl.pallas_call(
        functools.partial(_branch_a_kernel, tm=tm),
        grid=(seq // tm,),
        in_specs=[blk(offs["u"]), blk(offs["v"]), blk(offs["za"]),
                  const((1, w)), const((1, w)),
                  const((A_GROUPS, CHUNK, CHUNK)), const((A_GROUPS, CHUNK, CHUNK))],
        out_specs=pl.BlockSpec((tm, w), lambda i: (i, 0)),
        out_shape=jax.ShapeDtypeStruct((seq, w), BF16),
        compiler_params=_cparams("parallel"),
        name="branch_a",
    )(proj, proj, proj, ln_g, ln_b, w_s, b_s_b)


def _mla_prep_kernel(cq_ref, ckv_ref, kr_ref, tc_ref, ta_ref, tb_ref, qg_ref, kvg_ref,
                     wq_ref, wkv_ref, q_ref, k_ref, vt_ref):
    scale = QK_DIM ** -0.5 * LOG2E
    tab_c, tab_a, tab_b = tc_ref[...], ta_ref[...], tb_ref[...]
    cqn = _rms(cq_ref[...].astype(F32), qg_ref[...]).astype(BF16)
    ckvn = _rms(ckv_ref[...].astype(F32), kvg_ref[...]).astype(BF16)
    kpe = _rope(kr_ref[...].astype(F32), tab_c, tab_a, tab_b).astype(BF16)
    for h in range(MLA_HEADS):
        lo, mid, hi = h * HEAD_PAD, h * HEAD_PAD + LANE, (h + 1) * HEAD_PAD
        q = _dot(cqn, wq_ref[:, lo:hi]) * scale
        q_ref[:, lo:mid] = q[:, :LANE].astype(BF16)
        q_ref[:, mid:hi] = _rope(q[:, LANE:], tab_c, tab_a, tab_b).astype(BF16)
        kv = _dot(ckvn, wkv_ref[:, lo:hi])
        k_ref[:, lo:mid] = kv[:, :LANE].astype(BF16)
        k_ref[:, mid:hi] = kpe
        vt_ref[h * V_DIM:(h + 1) * V_DIM, :] = kv[:, LANE:].T.astype(BF16)


def _mla_prep(proj, tabs, q_norm_g, kv_norm_g, wq_pad, wkv, offs, tm):
    seq = proj.shape[0]
    hw = MLA_HEADS * HEAD_PAD
    const = lambda shape: pl.BlockSpec(shape, lambda i: (0,) * len(shape))
    col = lambda off, w: pl.BlockSpec((tm, w), lambda i: (i, off // w))
    tab = pl.BlockSpec((tm, LANE), lambda i: (i, 0))
    return pl.pallas_call(
        _mla_prep_kernel,
        grid=(seq // tm,),
        in_specs=[col(offs["cq"], Q_LORA), col(offs["ckv"], KV_LORA), col(offs["kr"], LANE),
                  tab, tab, tab, const((1, Q_LORA)), const((1, KV_LORA)),
                  const((Q_LORA, hw)), const((KV_LORA, hw))],
        out_specs=[pl.BlockSpec((tm, hw), lambda i: (i, 0)),
                   pl.BlockSpec((tm, hw), lambda i: (i, 0)),
                   pl.BlockSpec((MLA_HEADS * V_DIM, tm), lambda i: (0, i))],
        out_shape=[jax.ShapeDtypeStruct((seq, hw), BF16),
                   jax.ShapeDtypeStruct((seq, hw), BF16),
                   jax.ShapeDtypeStruct((MLA_HEADS * V_DIM, seq), BF16)],
        compiler_params=_cparams("parallel"),
        name="mla_prep",
    )(proj, proj, proj, *tabs, q_norm_g, kv_norm_g, wq_pad, wkv)


def _flash_kernel(q_ref, k_ref, vt_ref, g_ref, o_ref, *scratch, tq, tk, cb):
    qi = pl.program_id(1)
    ncb = tq // cb
    take = lambda n: [scratch[i * ncb:(i + 1) * ncb] for i in range(n)]
    m_refs, l_refs, acc_refs, s0, s1, mn0, mn1, al0, al1 = take(9)
    s_refs, mnew_refs, alpha_refs = (s0, s1), (mn0, mn1), (al0, al1)
    for c in range(ncb):
        m_refs[c][...] = jnp.full(m_refs[c].shape, -1e30, F32)
        l_refs[c][...] = jnp.zeros(l_refs[c].shape, F32)
        acc_refs[c][...] = jnp.zeros(acc_refs[c].shape, F32)

    def scores(c, kstart, rows, slot, masked):
        s = _dot_nt(k_ref[pl.ds(kstart, rows), :], q_ref[c * cb:(c + 1) * cb, :])
        if masked:
            key = lax.broadcasted_iota(jnp.int32, s.shape, 0)
            qry = lax.broadcasted_iota(jnp.int32, s.shape, 1)
            s = jnp.where(key <= qry, s, -1e30)
        m_prev = m_refs[c][...]
        m_new = jnp.maximum(m_prev, jnp.max(s, axis=0, keepdims=True))
        s_refs[slot][c][:rows, :] = s
        mnew_refs[slot][c][...] = m_new
        alpha_refs[slot][c][...] = jnp.exp2(m_prev - m_new)
        m_refs[c][...] = m_new

    def update(c, kstart, rows, slot):
        alpha = alpha_refs[slot][c][...]
        p = jnp.exp2(s_refs[slot][c][:rows, :] - mnew_refs[slot][c][...])
        l_refs[c][...] = alpha * l_refs[c][...] + jnp.sum(p, axis=0, keepdims=True)
        pv = _dot(vt_ref[:, pl.ds(kstart, rows)], p.astype(BF16))
        acc_refs[c][...] = alpha * acc_refs[c][...] + pv

    def kv(t):
        return pl.multiple_of(t * tk, tk)

    assert tq == 2 * tk

    @pl.when(qi > 0)
    def _():
        for c in range(ncb):
            scores(c, 0, tk, 0, False)

        def body(i, carry):
            for c in range(ncb):
                scores(c, kv(2 * i + 1), tk, 1, False)
                update(c, kv(2 * i), tk, 0)
            for c in range(ncb):
                scores(c, kv(2 * i + 2), tk, 0, False)
                update(c, kv(2 * i + 1), tk, 1)
            return carry

        lax.fori_loop(0, qi - 1, body, 0)
        for c in range(ncb):
            scores(c, kv(2 * qi - 1), tk, 1, False)
            update(c, kv(2 * qi - 2), tk, 0)
        for c in range(ncb):
            update(c, kv(2 * qi - 1), tk, 1)

    base = pl.multiple_of(qi * tq, tq)
    items = []
    for c in range(ncb):
        for off in range(0, c * cb, tk):
            items.append((c, off, min(tk, c * cb - off), False))
        items.append((c, c * cb, cb, True))
    for n in range(len(items) + 1):
        if n < len(items):
            c, off, rows, masked = items[n]
            scores(c, pl.multiple_of(base + off, cb), rows, n % 2, masked)
        if n > 0:
            pc, poff, prows, _ = items[n - 1]
            update(pc, pl.multiple_of(base + poff, cb), prows, (n - 1) % 2)

    for c in range(ncb):
        o = (acc_refs[c][...] / l_refs[c][...]).T
        rs = slice(c * cb, (c + 1) * cb)
        o_ref[rs, :] = (o * g_ref[rs, :].astype(F32)).astype(o_ref.dtype)


def _flash(q, k, vt, proj, zb_off, tq, tk, cb):
    seq = q.shape[0]
    ncb = tq // cb
    stat = [pltpu.VMEM((1, cb), F32)] * ncb
    scratch = (stat + stat + [pltpu.VMEM((V_DIM, cb), F32)] * ncb
               + [pltpu.VMEM((tk, cb), F32)] * (2 * ncb)
               + stat * 4)
    return pl.pallas_call(
        functools.partial(_flash_kernel, tq=tq, tk=tk, cb=cb),
        grid=(MLA_HEADS, seq // tq),
        in_specs=[pl.BlockSpec((tq, HEAD_PAD), lambda h, i: (i, h)),
                  pl.BlockSpec((seq, HEAD_PAD), lambda h, i: (0, h)),
                  pl.BlockSpec((V_DIM, seq), lambda h, i: (h, 0)),
                  pl.BlockSpec((tq, V_DIM), lambda h, i: (i, zb_off // V_DIM + h))],
        out_specs=pl.BlockSpec((tq, V_DIM), lambda h, i: (i, h)),
        out_shape=jax.ShapeDtypeStruct((seq, MLA_HEADS * V_DIM), BF16),
        scratch_shapes=scratch,
        compiler_params=_cparams("parallel", "arbitrary"),
        name="flash_mla",
    )(q, k, vt, proj)


def _mem_kv_kernel(mem_ref, g_ref, w_ref, o_ref):
    mn = _rms(mem_ref[...], g_ref[...]).astype(BF16)
    o_ref[...] = _dot(mn, w_ref[...]).astype(o_ref.dtype)


def _mem_kv(mem2, mem_norm_g, w_mem_kv, tn):
    m, d = mem2.shape
    n = w_mem_kv.shape[1]
    return pl.pallas_call(
        _mem_kv_kernel,
        grid=(n // tn,),
        in_specs=[pl.BlockSpec((m, d), lambda j: (0, 0)),
                  pl.BlockSpec((1, d), lambda j: (0, 0)),
                  pl.BlockSpec((d, tn), lambda j: (0, j))],
        out_specs=pl.BlockSpec((m, tn), lambda j: (0, j)),
        out_shape=jax.ShapeDtypeStruct((m, n), BF16),
        compiler_params=_cparams("parallel"),
        name="mem_kv",
    )(mem2, mem_norm_g, w_mem_kv)


def _mem_attn_kernel(q_ref, kv_ref, sz_ref, o_ref, *, width):
    hd = width // MEM_HEADS
    scale = hd ** -0.5
    for h in range(MEM_HEADS):
        cs = slice(h * hd, (h + 1) * hd)
        s = _dot_nt(q_ref[:, cs], kv_ref[:, cs]) * scale
        p = jnp.exp(s - jnp.max(s, axis=-1, keepdims=True))
        l = jnp.sum(p, axis=-1, keepdims=True)
        o = _dot(p.astype(BF16), kv_ref[:, width + h * hd:width + (h + 1) * hd]) / l
        o_ref[:, cs] = (o * sz_ref[:, cs].astype(F32)).astype(o_ref.dtype)


def _mem_attn(proj, kvm, offs, width, tm):
    seq = proj.shape[0]
    blk = lambda off: pl.BlockSpec((tm, width), lambda i: (i, off // width))
    return pl.pallas_call(
        functools.partial(_mem_attn_kernel, width=width),
        grid=(seq // tm,),
        in_specs=[blk(offs["qm"]), pl.BlockSpec(kvm.shape, lambda i: (0, 0)), blk(offs["zm"])],
        out_specs=pl.BlockSpec((tm, width), lambda i: (i, 0)),
        out_shape=jax.ShapeDtypeStruct((seq, width), BF16),
        compiler_params=_cparams("parallel"),
        name="mem_attn",
    )(proj, kvm, proj)


def _merge_kernel(ya_ref, yb_ref, ym_ref, w_ref, g0_ref, g1_ref, g2_ref, o_ref):
    acc = g0_ref[...].astype(F32) * _dot(ya_ref[...], w_ref[0])
    acc += g1_ref[...].astype(F32) * _dot(yb_ref[...], w_ref[1])
    acc += g2_ref[...].astype(F32) * _dot(ym_ref[...], w_ref[2])
    o_ref[...] = acc.astype(o_ref.dtype)


def _merge(y_a, y_b, y_m, w_branch, proj, gate_off, tm, tn):
    seq, width = y_a.shape
    d = w_branch.shape[2]
    yblk = pl.BlockSpec((tm, width), lambda i, j: (i, 0))
    gblk = lambda n: pl.BlockSpec((tm, tn), lambda i, j, n=n: (i, (gate_off + n * d) // tn + j))
    return pl.pallas_call(
        _merge_kernel,
        grid=(seq // tm, d // tn),
        in_specs=[yblk, yblk, yblk,
                  pl.BlockSpec((N_BRANCH, width, tn), lambda i, j: (0, 0, j)),
                  gblk(0), gblk(1), gblk(2)],
        out_specs=pl.BlockSpec((tm, tn), lambda i, j: (i, j)),
        out_shape=jax.ShapeDtypeStruct((seq, d), BF16),
        compiler_params=_cparams("parallel", "arbitrary"),
        name="merge",
    )(y_a, y_b, y_m, w_branch, proj, proj, proj)


def _out_kernel(m_ref, w_ref, x_ref, g_ref, o_ref):
    out = _dot(m_ref[...], w_ref[...])
    o_ref[...] = x_ref[...] + _rms(out, g_ref[...])


def _out_proj(merged, w_out, x2, g_post, tm):
    seq, d = x2.shape
    return pl.pallas_call(
        _out_kernel,
        grid=(seq // tm,),
        in_specs=[pl.BlockSpec((tm, d), lambda i: (i, 0)),
                  pl.BlockSpec((d, d), lambda i: (0, 0)),
                  pl.BlockSpec((tm, d), lambda i: (i, 0)),
                  pl.BlockSpec((1, d), lambda i: (0, 0))],
        out_specs=pl.BlockSpec((tm, d), lambda i: (i, 0)),
        out_shape=jax.ShapeDtypeStruct((seq, d), F32),
        compiler_params=_cparams("parallel"),
        name="out_proj",
    )(merged, w_out, x2, g_post)


def _layer(x2, mem2, tabs, g_pre, w_in, a_ln_g, a_ln_b, a_w_s, a_b_s, q_norm_g, w_uq,
           kv_norm_g, w_ukv, mem_norm_g, w_mem_kv, w_gate, b_gate, w_branch, w_out, g_post):
    seq, d = x2.shape
    tn = 512
    sizes = (d, d, d, Q_LORA, KV_LORA, QK_ROPE, d, d)
    u, v, za, cq, ckv, kr, zb, qm, zm = jnp.split(
        w_in, [sum(sizes[:n + 1]) for n in range(len(sizes))], axis=1)
    kr_pad = jnp.zeros((d, tn - QK_ROPE), w_in.dtype)
    w_all = jnp.concatenate([u, v, za, zb, zm, w_gate, qm, cq, ckv, kr, kr_pad], axis=1).astype(BF16)
    offs = {"u": 0, "v": d, "za": 2 * d, "zb": 3 * d, "zm": 4 * d, "gate": 5 * d,
            "qm": 8 * d, "cq": 9 * d, "ckv": 9 * d + Q_LORA, "kr": 9 * d + Q_LORA + KV_LORA}
    n_all = w_all.shape[1]
    b_all = jnp.zeros((1, n_all), F32).at[0, offs["gate"]:offs["gate"] + N_BRANCH * d].set(b_gate)
    bounds = (offs["za"] // tn, offs["gate"] // tn, offs["qm"] // tn, n_all // tn)

    proj = _projection(x2, g_pre.reshape(1, d), w_all, b_all, bounds, tm=1024, tn=tn)

    b_s_b = jnp.broadcast_to(a_b_s[:, :, None], (A_GROUPS, CHUNK, CHUNK))
    y_a = _branch_a(proj, a_ln_g.reshape(1, d), a_ln_b.reshape(1, d), a_w_s, b_s_b, offs, tm=256)

    wq_pad = jnp.pad(w_uq.reshape(Q_LORA, MLA_HEADS, QK_DIM),
                     ((0, 0), (0, 0), (0, HEAD_PAD - QK_DIM))).reshape(Q_LORA, -1).astype(BF16)
    q, k, vv = _mla_prep(proj, tabs, q_norm_g.reshape(1, -1), kv_norm_g.reshape(1, -1),
                         wq_pad, w_ukv.astype(BF16), offs, tm=512)
    y_b = _flash(q, k, vv, proj, offs["zb"], tq=1024, tk=512, cb=256)

    kvm = _mem_kv(mem2, mem_norm_g.reshape(1, d), w_mem_kv.astype(BF16), tn=1024)
    y_m = _mem_attn(proj, kvm, offs, d, tm=512)

    merged = _merge(y_a, y_b, y_m, w_branch.astype(BF16), proj, offs["gate"], tm=512, tn=512)
    return _out_proj(merged, w_out.astype(BF16), x2, g_post.reshape(1, d), tm=512)


def kernel(x, mem, positions, g_pre, w_in, a_ln_g, a_ln_b, a_w_s, a_b_s, q_norm_g, w_uq,
           kv_norm_g, w_ukv, mem_norm_g, w_mem_kv, w_gate, b_gate, w_branch, w_out, g_post):
    batch, seq, d = x.shape
    depth = g_pre.shape[0]
    outs = []
    for b in range(batch):
        tabs = _rope_tables(positions[b], seq)
        xb = x[b]
        for l in range(depth):
            xb = _layer(xb, mem[b], tabs, g_pre[l], w_in[l], a_ln_g[l], a_ln_b[l], a_w_s[l],
                        a_b_s[l], q_norm_g[l], w_uq[l], kv_norm_g[l], w_ukv[l], mem_norm_g[l],
                        w_mem_kv[l], w_gate[l], b_gate[l], w_branch[l], w_out[l], g_post[l])
        outs.append(xb)
    return jnp.stack(outs, axis=0)
```

```python
import functools

import jax
import jax.numpy as jnp
from jax import lax
from jax.experimental import pallas as pl
from jax.experimental.pallas import tpu as pltpu

EPS = 1e-6
CHUNK = 128
A_GROUPS = 16
QK_NOPE = 128
QK_ROPE = 64
V_DIM = 128
MLA_HEADS = 16
Q_LORA = 512
KV_LORA = 512
QK_DIM = QK_NOPE + QK_ROPE
ROPE_THETA = 10000.0
LOG2E = 1.4426950408889634
MEM_HEADS = 4
N_BRANCH = 3
LANE = 128
HEAD_PAD = 256
VMEM_LIMIT = 52 * 1024 * 1024

F32 = jnp.float32
BF16 = jnp.bfloat16


def _cparams(*sem):
    return pltpu.CompilerParams(dimension_semantics=sem, vmem_limit_bytes=VMEM_LIMIT)


def _dot(a, b):
    return jnp.dot(a, b, preferred_element_type=F32)


def _dot_nt(a, b):
    return lax.dot_general(a, b, (((1,), (1,)), ((), ())), preferred_element_type=F32)


def _gelu(x):
    return 0.5 * x * (1.0 + jnp.tanh(0.7978845608028654 * (x + 0.044715 * (x * x * x))))


def _sigmoid(x):
    return 1.0 / (1.0 + jnp.exp(-x))


def _silu(x):
    return x * _sigmoid(x)


def _identity(x):
    return x


def _rms(xf, g):
    return xf * lax.rsqrt(jnp.mean(xf * xf, axis=-1, keepdims=True) + EPS) * g


def _rope_table_kernel(pos_ref, invf_ref, cos_ref, sin_ref):
    ang = pos_ref[...].astype(F32) * invf_ref[...]
    cos_ref[...] = jnp.cos(ang)
    sin_ref[...] = jnp.sin(ang)


def _rope_tables(positions, seq):
    half = QK_ROPE // 2
    inv_freq = 1.0 / (ROPE_THETA ** (jnp.arange(0, QK_ROPE, 2, dtype=F32) / QK_ROPE))
    cos_t, sin_t = pl.pallas_call(
        _rope_table_kernel,
        out_shape=(jax.ShapeDtypeStruct((half, seq), F32),) * 2,
        name="rope_tables",
    )(positions.reshape(1, seq), inv_freq.reshape(half, 1))
    cos, sin = cos_t.T, sin_t.T
    z32 = jnp.zeros((seq, half), F32)
    z64 = jnp.zeros((seq, 2 * half), F32)
    tab_c = jnp.concatenate([cos, cos, z64], axis=-1)
    tab_a = jnp.concatenate([sin, z32, z64], axis=-1)
    tab_b = jnp.concatenate([z32, sin, z64], axis=-1)
    return tab_c, tab_a, tab_b


def _rope(t, tab_c, tab_a, tab_b):
    down = pltpu.roll(t, 3 * LANE // 4, axis=1)
    up = pltpu.roll(t, LANE // 4, axis=1)
    return t * tab_c - down * tab_a + up * tab_b


def _pack_w_kernel(a_ref, b_ref, g_ref, o_ref, *, n_lead, n_shift, n_gate, shift):
    j = pl.program_id(0)
    in_shift = (j >= n_lead) & (j < n_lead + n_shift)
    in_gate = (j >= n_lead + n_shift) & (j < n_lead + n_shift + n_gate)

    @pl.when(in_shift)
    def _():
        o_ref[...] = jnp.concatenate([a_ref[:, shift:], b_ref[:, :shift]], axis=1).astype(BF16)

    @pl.when(in_gate)
    def _():
        o_ref[...] = g_ref[...].astype(BF16)

    @pl.when(jnp.logical_not(in_shift | in_gate))
    def _():
        o_ref[...] = a_ref[...].astype(BF16)


def _pack_weights(w_in, w_gate, tn):
    d = w_in.shape[0]
    n_lead = 3 * d // tn
    n_shift = 3 * d // tn
    n_gate = w_gate.shape[1] // tn
    n_lat = (Q_LORA + KV_LORA) // tn
    lat_blk = 3 * d // tn
    tail_blk = (3 * d + Q_LORA + KV_LORA) // tn
    last = n_lead + n_shift + n_gate
    per = tn // LANE

    def a_idx(j):
        return jnp.where(j < n_lead, j,
                         jnp.where(j < n_lead + n_shift, j - n_lead + tail_blk,
                                   jnp.where(j < last, tail_blk + n_shift - 1, j - last + lat_blk)))

    def b_idx(j):
        return jnp.clip(j - n_lead, 0, n_shift - 1) * per + (tail_blk + 1) * per

    return pl.pallas_call(
        functools.partial(_pack_w_kernel, n_lead=n_lead, n_shift=n_shift, n_gate=n_gate,
                          shift=QK_ROPE),
        grid=(last + n_lat,),
        in_specs=[pl.BlockSpec((d, tn), lambda j: (0, a_idx(j))),
                  pl.BlockSpec((d, LANE), lambda j: (0, b_idx(j))),
                  pl.BlockSpec((d, tn), lambda j: (0, jnp.clip(j - n_lead - n_shift, 0, n_gate - 1)))],
        out_specs=pl.BlockSpec((d, tn), lambda j: (0, j)),
        out_shape=jax.ShapeDtypeStruct((d, (last + n_lat) * tn), BF16),
        compiler_params=_cparams("arbitrary"),
        name="pack_w",
    )(w_in, w_in, w_gate)


def _proj_kernel(x_ref, g_ref, w_ref, b_ref, o_ref, h_ref, *, acts):
    j = pl.program_id(1)

    @pl.when(j == 0)
    def _():
        h_ref[...] = _rms(x_ref[...], g_ref[...]).astype(BF16)

    for act in sorted(set(acts), key=lambda f: f.__name__):
        pred = functools.reduce(jnp.logical_or, [j == t for t, a in enumerate(acts) if a is act])

        @pl.when(pred)
        def _(act=act):
            acc = _dot(h_ref[...], w_ref[...]) + b_ref[...]
            o_ref[...] = act(acc).astype(o_ref.dtype)


def _projection(x2, g_pre, w_all, b_all, acts, tm, tn):
    seq, d = x2.shape
    n = w_all.shape[1]
    return pl.pallas_call(
        functools.partial(_proj_kernel, acts=acts),
        grid=(seq // tm, n // tn),
        in_specs=[
            pl.BlockSpec((tm, d), lambda i, j: (i, 0)),
            pl.BlockSpec((1, d), lambda i, j: (0, 0)),
            pl.BlockSpec((d, tn), lambda i, j: (0, j)),
            pl.BlockSpec((1, tn), lambda i, j: (0, j)),
        ],
        out_specs=[pl.BlockSpec((tm, tn), lambda i, j: (i, j)),
                   pl.BlockSpec((tm, d), lambda i, j: (i, 0))],
        out_shape=[jax.ShapeDtypeStruct((seq, n), BF16), jax.ShapeDtypeStruct((seq, d), BF16)],
        compiler_params=_cparams("parallel", "arbitrary"),
        name="in_proj",
    )(x2, g_pre, w_all, b_all)


def _branch_a_kernel(gu_ref, gv_ref, sz_ref, lng_ref, lnb_ref, ws_ref, bs_ref, o_ref, *, tm):
    gv = gv_ref[...].astype(F32)
    mu = jnp.mean(gv, axis=-1, keepdims=True)
    cen = gv - mu
    var = jnp.mean(cen * cen, axis=-1, keepdims=True)
    vn = (cen * lax.rsqrt(var + EPS) * lng_ref[...] + lnb_ref[...]).astype(BF16)
    row = lax.broadcasted_iota(jnp.int32, (CHUNK, CHUNK), 0)
    col = lax.broadcasted_iota(jnp.int32, (CHUNK, CHUNK), 1)
    causal = col <= row
    for g in range(A_GROUPS):
        ws = jnp.where(causal, ws_ref[g], 0.0).astype(BF16)
        cs = slice(g * CHUNK, (g + 1) * CHUNK)
        for c in range(tm // CHUNK):
            rs = slice(c * CHUNK, (c + 1) * CHUNK)
            sv = _dot(ws, vn[rs, cs]) + bs_ref[g]
            y = gu_ref[rs, cs].astype(F32) * sv * sz_ref[rs, cs].astype(F32)
            o_ref[rs, cs] = y.astype(o_ref.dtype)


def _branch_a(proj, ln_g, ln_b, w_s, b_s_b, offs, tm):
    seq = proj.shape[0]
    w = A_GROUPS * CHUNK
    blk = lambda off: pl.BlockSpec((tm, w), lambda i, off=off: (i, off // w))
    const = lambda shape: pl.BlockSpec(shape, lambda i: (0,) * len(shape))
    return pl.pallas_call(
        functools.partial(_branch_a_kernel, tm=tm),
        grid=(seq // tm,),
        in_specs=[blk(offs["u"]), blk(offs["v"]), blk(offs["za"]),
                  const((1, w)), const((1, w)),
                  const((A_GROUPS, CHUNK, CHUNK)), const((A_GROUPS, CHUNK, CHUNK))],
        out_specs=pl.BlockSpec((tm, w), lambda i: (i, 0)),
        out_shape=jax.ShapeDtypeStruct((seq, w), BF16),
        compiler_params=_cparams("parallel"),
        name="branch_a",
    )(proj, proj, proj, ln_g, ln_b, w_s, b_s_b)


def _mla_prep_kernel(cq_ref, ckv_ref, h_ref, wkr_ref, tc_ref, ta_ref, tb_ref, qg_ref, kvg_ref,
                     wq_ref, wkv_ref, q_ref, k_ref, vt_ref):
    scale = QK_DIM ** -0.5 * LOG2E
    tab_c, tab_a, tab_b = tc_ref[...], ta_ref[...], tb_ref[...]
    cqn = _rms(cq_ref[...].astype(F32), qg_ref[...]).astype(BF16)
    ckvn = _rms(ckv_ref[...].astype(F32), kvg_ref[...]).astype(BF16)
    kpe = _rope(_dot(h_ref[...], wkr_ref[...]), tab_c, tab_a, tab_b).astype(BF16)
    for h in range(MLA_HEADS):
        lo, mid, hi = h * HEAD_PAD, h * HEAD_PAD + LANE, (h + 1) * HEAD_PAD
        q = _dot(cqn, wq_ref[:, lo:hi]) * scale
        q_ref[:, lo:mid] = q[:, :LANE].astype(BF16)
        q_ref[:, mid:hi] = _rope(q[:, LANE:], tab_c, tab_a, tab_b).astype(BF16)
        kv = _dot(ckvn, wkv_ref[:, lo:hi])
        k_ref[:, lo:mid] = kv[:, :LANE].astype(BF16)
        k_ref[:, mid:hi] = kpe
        vt_ref[h * V_DIM:(h + 1) * V_DIM, :] = kv[:, LANE:].T.astype(BF16)


def _mla_prep(proj, h, w_kr, tabs, q_norm_g, kv_norm_g, wq_pad, wkv, offs, tm):
    seq, d = h.shape
    hw = MLA_HEADS * HEAD_PAD
    const = lambda shape: pl.BlockSpec(shape, lambda i: (0,) * len(shape))
    col = lambda off, w: pl.BlockSpec((tm, w), lambda i: (i, off // w))
    tab = pl.BlockSpec((tm, LANE), lambda i: (i, 0))
    return pl.pallas_call(
        _mla_prep_kernel,
        grid=(seq // tm,),
        in_specs=[col(offs["cq"], Q_LORA), col(offs["ckv"], KV_LORA),
                  pl.BlockSpec((tm, d), lambda i: (i, 0)), const((d, LANE)),
                  tab, tab, tab, const((1, Q_LORA)), const((1, KV_LORA)),
                  const((Q_LORA, hw)), const((KV_LORA, hw))],
        out_specs=[pl.BlockSpec((tm, hw), lambda i: (i, 0)),
                   pl.BlockSpec((tm, hw), lambda i: (i, 0)),
                   pl.BlockSpec((MLA_HEADS * V_DIM, tm), lambda i: (0, i))],
        out_shape=[jax.ShapeDtypeStruct((seq, hw), BF16),
                   jax.ShapeDtypeStruct((seq, hw), BF16),
                   jax.ShapeDtypeStruct((MLA_HEADS * V_DIM, seq), BF16)],
        compiler_params=_cparams("parallel"),
        name="mla_prep",
    )(proj, proj, h, w_kr, *tabs, q_norm_g, kv_norm_g, wq_pad, wkv)


def _flash_kernel(q_ref, k_ref, vt_ref, g_ref, o_ref, *scratch, tq, tk, cb):
    qi = pl.program_id(1)
    ncb = tq // cb
    take = lambda n: [scratch[i * ncb:(i + 1) * ncb] for i in range(n)]
    m_refs, l_refs, acc_refs, s0, s1, mn0, mn1, al0, al1 = take(9)
    s_refs, mnew_refs, alpha_refs = (s0, s1), (mn0, mn1), (al0, al1)
    for c in range(ncb):
        m_refs[c][...] = jnp.full(m_refs[c].shape, -1e30, F32)
        l_refs[c][...] = jnp.zeros(l_refs[c].shape, F32)
        acc_refs[c][...] = jnp.zeros(acc_refs[c].shape, F32)

    def scores(c, kstart, rows, slot, masked):
        s = _dot_nt(k_ref[pl.ds(kstart, rows), :], q_ref[c * cb:(c + 1) * cb, :])
        if masked:
            key = lax.broadcasted_iota(jnp.int32, s.shape, 0)
            qry = lax.broadcasted_iota(jnp.int32, s.shape, 1)
            s = jnp.where(key <= qry, s, -1e30)
        m_prev = m_refs[c][...]
        m_new = jnp.maximum(m_prev, jnp.max(s, axis=0, keepdims=True))
        s_refs[slot][c][:rows, :] = s
        mnew_refs[slot][c][...] = m_new
        alpha_refs[slot][c][...] = jnp.exp2(m_prev - m_new)
        m_refs[c][...] = m_new

    def update(c, kstart, rows, slot):
        alpha = alpha_refs[slot][c][...]
        p = jnp.exp2(s_refs[slot][c][:rows, :] - mnew_refs[slot][c][...])
        l_refs[c][...] = alpha * l_refs[c][...] + jnp.sum(p, axis=0, keepdims=True)
        pv = _dot(vt_ref[:, pl.ds(kstart, rows)], p.astype(BF16))
        acc_refs[c][...] = alpha * acc_refs[c][...] + pv

    def kv(t):
        return pl.multiple_of(t * tk, tk)

    assert tq == 2 * tk

    @pl.when(qi > 0)
    def _():
        for c in range(ncb):
            scores(c, 0, tk, 0, False)

        def body(i, carry):
            for c in range(ncb):
                scores(c, kv(2 * i + 1), tk, 1, False)
                update(c, kv(2 * i), tk, 0)
            for c in range(ncb):
                scores(c, kv(2 * i + 2), tk, 0, False)
                update(c, kv(2 * i + 1), tk, 1)
            return carry

        lax.fori_loop(0, qi - 1, body, 0)
        for c in range(ncb):
            scores(c, kv(2 * qi - 1), tk, 1, False)
            update(c, kv(2 * qi - 2), tk, 0)
        for c in range(ncb):
            update(c, kv(2 * qi - 1), tk, 1)

    base = pl.multiple_of(qi * tq, tq)
    items = []
    for c in range(ncb):
        for off in range(0, c * cb, tk):
            items.append((c, off, min(tk, c * cb - off), False))
        items.append((c, c * cb, cb, True))
    for n in range(len(items) + 1):
        if n < len(items):
            c, off, rows, masked = items[n]
            scores(c, pl.multiple_of(base + off, cb), rows, n % 2, masked)
        if n > 0:
            pc, poff, prows, _ = items[n - 1]
            update(pc, pl.multiple_of(base + poff, cb), prows, (n - 1) % 2)

    for c in range(ncb):
        o = (acc_refs[c][...] / l_refs[c][...]).T
        rs = slice(c * cb, (c + 1) * cb)
        o_ref[rs, :] = (o * g_ref[rs, :].astype(F32)).astype(o_ref.dtype)


def _flash(q, k, vt, proj, zb_off, tq, tk, cb):
    seq = q.shape[0]
    ncb = tq // cb
    stat = [pltpu.VMEM((1, cb), F32)] * ncb
    scratch = (stat + stat + [pltpu.VMEM((V_DIM, cb), F32)] * ncb
               + [pltpu.VMEM((tk, cb), F32)] * (2 * ncb)
               + stat * 4)
    return pl.pallas_call(
        functools.partial(_flash_kernel, tq=tq, tk=tk, cb=cb),
        grid=(MLA_HEADS, seq // tq),
        in_specs=[pl.BlockSpec((tq, HEAD_PAD), lambda h, i: (i, h)),
                  pl.BlockSpec((seq, HEAD_PAD), lambda h, i: (0, h)),
                  pl.BlockSpec((V_DIM, seq), lambda h, i: (h, 0)),
                  pl.BlockSpec((tq, V_DIM), lambda h, i: (i, zb_off // V_DIM + h))],
        out_specs=pl.BlockSpec((tq, V_DIM), lambda h, i: (i, h)),
        out_shape=jax.ShapeDtypeStruct((seq, MLA_HEADS * V_DIM), BF16),
        scratch_shapes=scratch,
        compiler_params=_cparams("parallel", "arbitrary"),
        name="flash_mla",
    )(q, k, vt, proj)


def _mem_kv_kernel(mem_ref, g_ref, w_ref, o_ref):
    mn = _rms(mem_ref[...], g_ref[...]).astype(BF16)
    o_ref[...] = _dot(mn, w_ref[...]).astype(o_ref.dtype)


def _mem_kv(mem2, mem_norm_g, w_mem_kv, tn):
    m, d = mem2.shape
    n = w_mem_kv.shape[1]
    return pl.pallas_call(
        _mem_kv_kernel,
        grid=(n // tn,),
        in_specs=[pl.BlockSpec((m, d), lambda j: (0, 0)),
                  pl.BlockSpec((1, d), lambda j: (0, 0)),
                  pl.BlockSpec((d, tn), lambda j: (0, j))],
        out_specs=pl.BlockSpec((m, tn), lambda j: (0, j)),
        out_shape=jax.ShapeDtypeStruct((m, n), BF16),
        compiler_params=_cparams("parallel"),
        name="mem_kv",
    )(mem2, mem_norm_g, w_mem_kv)


def _mem_attn_kernel(q_ref, kv_ref, sz_ref, o_ref, *, width):
    hd = width // MEM_HEADS
    scale = hd ** -0.5
    for h in range(MEM_HEADS):
        cs = slice(h * hd, (h + 1) * hd)
        s = _dot_nt(q_ref[:, cs], kv_ref[:, cs]) * scale
        p = jnp.exp(s - jnp.max(s, axis=-1, keepdims=True))
        l = jnp.sum(p, axis=-1, keepdims=True)
        o = _dot(p.astype(BF16), kv_ref[:, width + h * hd:width + (h + 1) * hd]) / l
        o_ref[:, cs] = (o * sz_ref[:, cs].astype(F32)).astype(o_ref.dtype)


def _mem_attn(proj, kvm, offs, width, tm):
    seq = proj.shape[0]
    blk = lambda off: pl.BlockSpec((tm, width), lambda i: (i, off // width))
    return pl.pallas_call(
        functools.partial(_mem_attn_kernel, width=width),
        grid=(seq // tm,),
        in_specs=[blk(offs["qm"]), pl.BlockSpec(kvm.shape, lambda i: (0, 0)), blk(offs["zm"])],
        out_specs=pl.BlockSpec((tm, width), lambda i: (i, 0)),
        out_shape=jax.ShapeDtypeStruct((seq, width), BF16),
        compiler_params=_cparams("parallel"),
        name="mem_attn",
    )(proj, kvm, proj)


def _merge_kernel(ya_ref, yb_ref, ym_ref, w_ref, g0_ref, g1_ref, g2_ref, o_ref):
    acc = g0_ref[...].astype(F32) * _dot(ya_ref[...], w_ref[0])
    acc += g1_ref[...].astype(F32) * _dot(yb_ref[...], w_ref[1])
    acc += g2_ref[...].astype(F32) * _dot(ym_ref[...], w_ref[2])
    o_ref[...] = acc.astype(o_ref.dtype)


def _merge(y_a, y_b, y_m, w_branch, proj, gate_off, tm, tn):
    seq, width = y_a.shape
    d = w_branch.shape[2]
    yblk = pl.BlockSpec((tm, width), lambda i, j: (i, 0))
    gblk = lambda n: pl.BlockSpec((tm, tn), lambda i, j, n=n: (i, (gate_off + n * d) // tn + j))
    return pl.pallas_call(
        _merge_kernel,
        grid=(seq // tm, d // tn),
        in_specs=[yblk, yblk, yblk,
                  pl.BlockSpec((N_BRANCH, width, tn), lambda i, j: (0, 0, j)),
                  gblk(0), gblk(1), gblk(2)],
        out_specs=pl.BlockSpec((tm, tn), lambda i, j: (i, j)),
        out_shape=jax.ShapeDtypeStruct((seq, d), BF16),
        compiler_params=_cparams("parallel", "arbitrary"),
        name="merge",
    )(y_a, y_b, y_m, w_branch, proj, proj, proj)


def _out_kernel(m_ref, w_ref, x_ref, g_ref, o_ref):
    out = _dot(m_ref[...], w_ref[...])
    o_ref[...] = x_ref[...] + _rms(out, g_ref[...])


def _out_proj(merged, w_out, x2, g_post, tm):
    seq, d = x2.shape
    return pl.pallas_call(
        _out_kernel,
        grid=(seq // tm,),
        in_specs=[pl.BlockSpec((tm, d), lambda i: (i, 0)),
                  pl.BlockSpec((d, d), lambda i: (0, 0)),
                  pl.BlockSpec((tm, d), lambda i: (i, 0)),
                  pl.BlockSpec((1, d), lambda i: (0, 0))],
        out_specs=pl.BlockSpec((tm, d), lambda i: (i, 0)),
        out_shape=jax.ShapeDtypeStruct((seq, d), F32),
        compiler_params=_cparams("parallel"),
        name="out_proj",
    )(merged, w_out, x2, g_post)


def _layer(x2, mem2, tabs, g_pre, w_in, a_ln_g, a_ln_b, a_w_s, a_b_s, q_norm_g, w_uq,
           kv_norm_g, w_ukv, mem_norm_g, w_mem_kv, w_gate, b_gate, w_branch, w_out, g_post):
    seq, d = x2.shape
    tn = 1024
    w_all = _pack_weights(w_in, w_gate, tn=512)
    offs = {"u": 0, "v": d, "za": 2 * d, "zb": 3 * d, "qm": 4 * d, "zm": 5 * d, "gate": 6 * d,
            "cq": 9 * d, "ckv": 9 * d + Q_LORA}
    n_all = w_all.shape[1]
    b_all = jnp.zeros((1, n_all), F32).at[0, offs["gate"]:offs["gate"] + N_BRANCH * d].set(b_gate)
    sections = (("u", _gelu), ("v", _gelu), ("za", _silu), ("zb", _silu), ("qm", _identity),
                ("zm", _silu), ("gate", _sigmoid), ("cq", _identity))
    starts = [offs[name] // tn for name, _ in sections] + [n_all // tn]
    acts = tuple(act for (_, act), lo, hi in zip(sections, starts[:-1], starts[1:])
                 for _ in range(hi - lo))

    proj, h = _projection(x2, g_pre.reshape(1, d), w_all, b_all, acts, tm=1024, tn=tn)
    kr_lo = 3 * d + Q_LORA + KV_LORA
    w_kr = jnp.pad(w_in[:, kr_lo:kr_lo + QK_ROPE], ((0, 0), (0, LANE - QK_ROPE))).astype(BF16)

    b_s_b = jnp.broadcast_to(a_b_s[:, :, None], (A_GROUPS, CHUNK, CHUNK))
    y_a = _branch_a(proj, a_ln_g.reshape(1, d), a_ln_b.reshape(1, d), a_w_s, b_s_b, offs, tm=256)

    wq_pad = jnp.pad(w_uq.reshape(Q_LORA, MLA_HEADS, QK_DIM),
                     ((0, 0), (0, 0), (0, HEAD_PAD - QK_DIM))).reshape(Q_LORA, -1).astype(BF16)
    q, k, vv = _mla_prep(proj, h, w_kr, tabs, q_norm_g.reshape(1, -1), kv_norm_g.reshape(1, -1),
                         wq_pad, w_ukv.astype(BF16), offs, tm=512)
    y_b = _flash(q, k, vv, proj, offs["zb"], tq=1024, tk=512, cb=256)

    kvm = _mem_kv(mem2, mem_norm_g.reshape(1, d), w_mem_kv.astype(BF16), tn=1024)
    y_m = _mem_attn(proj, kvm, offs, d, tm=512)

    merged = _merge(y_a, y_b, y_m, w_branch.astype(BF16), proj, offs["gate"], tm=1024, tn=512)
    return _out_proj(merged, w_out.astype(BF16), x2, g_post.reshape(1, d), tm=512)


def kernel(x, mem, positions, g_pre, w_in, a_ln_g, a_ln_b, a_w_s, a_b_s, q_norm_g, w_uq,
           kv_norm_g, w_ukv, mem_norm_g, w_mem_kv, w_gate, b_gate, w_branch, w_out, g_post):
    batch, seq, d = x.shape
    depth = g_pre.shape[0]
    outs = []
    for b in range(batch):
        tabs = _rope_tables(positions[b], seq)
        xb = x[b]
        for l in range(depth):
            xb = _layer(xb, mem[b], tabs, g_pre[l], w_in[l], a_ln_g[l], a_ln_b[l], a_w_s[l],
                        a_b_s[l], q_norm_g[l], w_uq[l], kv_norm_g[l], w_ukv[l], mem_norm_g[l],
                        w_mem_kv[l], w_gate[l], b_gate[l], w_branch[l], w_out[l], g_post[l])
        outs.append(xb)
    return jnp.stack(outs, axis=0)
```

```python
import functools

import jax
import jax.numpy as jnp
from jax import lax
from jax.experimental import pallas as pl
from jax.experimental.pallas import tpu as pltpu

EPS = 1e-6
CHUNK = 128
A_GROUPS = 16
QK_NOPE = 128
QK_ROPE = 64
V_DIM = 128
MLA_HEADS = 16
Q_LORA = 512
KV_LORA = 512
QK_DIM = QK_NOPE + QK_ROPE
ROPE_THETA = 10000.0
LOG2E = 1.4426950408889634
MEM_HEADS = 4
N_BRANCH = 3
LANE = 128
HEAD_PAD = 256
VMEM_LIMIT = 52 * 1024 * 1024

F32 = jnp.float32
BF16 = jnp.bfloat16


def _cparams(*sem):
    return pltpu.CompilerParams(dimension_semantics=sem, vmem_limit_bytes=VMEM_LIMIT)


def _dot(a, b):
    return jnp.dot(a, b, preferred_element_type=F32)


def _dot_nt(a, b):
    return lax.dot_general(a, b, (((1,), (1,)), ((), ())), preferred_element_type=F32)


def _gelu(x):
    return 0.5 * x * (1.0 + jnp.tanh(0.7978845608028654 * (x + 0.044715 * (x * x * x))))


def _sigmoid(x):
    return 1.0 / (1.0 + jnp.exp(-x))


def _silu(x):
    return x * _sigmoid(x)


def _identity(x):
    return x


def _rms(xf, g):
    return xf * lax.rsqrt(jnp.mean(xf * xf, axis=-1, keepdims=True) + EPS) * g


def _rope_table_kernel(pos_ref, invf_ref, cos_ref, sin_ref):
    ang = pos_ref[...].astype(F32) * invf_ref[...]
    cos_ref[...] = jnp.cos(ang)
    sin_ref[...] = jnp.sin(ang)


def _rope_tables(positions, seq):
    half = QK_ROPE // 2
    inv_freq = 1.0 / (ROPE_THETA ** (jnp.arange(0, QK_ROPE, 2, dtype=F32) / QK_ROPE))
    cos_t, sin_t = pl.pallas_call(
        _rope_table_kernel,
        out_shape=(jax.ShapeDtypeStruct((half, seq), F32),) * 2,
        name="rope_tables",
    )(positions.reshape(1, seq), inv_freq.reshape(half, 1))
    cos, sin = cos_t.T, sin_t.T
    z32 = jnp.zeros((seq, half), F32)
    z64 = jnp.zeros((seq, 2 * half), F32)
    tab_c = jnp.concatenate([cos, cos, z64], axis=-1)
    tab_a = jnp.concatenate([sin, z32, z64], axis=-1)
    tab_b = jnp.concatenate([z32, sin, z64], axis=-1)
    return tab_c, tab_a, tab_b


def _rope(t, tab_c, tab_a, tab_b):
    down = pltpu.roll(t, 3 * LANE // 4, axis=1)
    up = pltpu.roll(t, LANE // 4, axis=1)
    return t * tab_c - down * tab_a + up * tab_b


def _pack_w_kernel(a_ref, b_ref, g_ref, o_ref, kr_ref, *, n_lead, n_shift, n_gate, shift):
    j = pl.program_id(0)
    in_shift = (j >= n_lead) & (j < n_lead + n_shift)
    in_gate = (j >= n_lead + n_shift) & (j < n_lead + n_shift + n_gate)

    @pl.when(j == n_lead)
    def _():
        lane = lax.broadcasted_iota(jnp.int32, kr_ref.shape, 1)
        kr_ref[...] = jnp.where(lane < shift, a_ref[:, :LANE], 0.0).astype(BF16)

    @pl.when(in_shift)
    def _():
        o_ref[...] = jnp.concatenate([a_ref[:, shift:], b_ref[:, :shift]], axis=1).astype(BF16)

    @pl.when(in_gate)
    def _():
        o_ref[...] = g_ref[...].astype(BF16)

    @pl.when(jnp.logical_not(in_shift | in_gate))
    def _():
        o_ref[...] = a_ref[...].astype(BF16)


def _pack_weights(w_in, w_gate, tn):
    d = w_in.shape[0]
    n_lead = 3 * d // tn
    n_shift = 3 * d // tn
    n_gate = w_gate.shape[1] // tn
    n_lat = (Q_LORA + KV_LORA) // tn
    lat_blk = 3 * d // tn
    tail_blk = (3 * d + Q_LORA + KV_LORA) // tn
    last = n_lead + n_shift + n_gate
    per = tn // LANE

    def a_idx(j):
        return jnp.where(j < n_lead, j,
                         jnp.where(j < n_lead + n_shift, j - n_lead + tail_blk,
                                   jnp.where(j < last, tail_blk + n_shift - 1, j - last + lat_blk)))

    def b_idx(j):
        return jnp.clip(j - n_lead, 0, n_shift - 1) * per + (tail_blk + 1) * per

    return pl.pallas_call(
        functools.partial(_pack_w_kernel, n_lead=n_lead, n_shift=n_shift, n_gate=n_gate,
                          shift=QK_ROPE),
        grid=(last + n_lat,),
        in_specs=[pl.BlockSpec((d, tn), lambda j: (0, a_idx(j))),
                  pl.BlockSpec((d, LANE), lambda j: (0, b_idx(j))),
                  pl.BlockSpec((d, tn), lambda j: (0, jnp.clip(j - n_lead - n_shift, 0, n_gate - 1)))],
        out_specs=[pl.BlockSpec((d, tn), lambda j: (0, j)),
                   pl.BlockSpec((d, LANE), lambda j: (0, 0))],
        out_shape=[jax.ShapeDtypeStruct((d, (last + n_lat) * tn), BF16),
                   jax.ShapeDtypeStruct((d, LANE), BF16)],
        compiler_params=_cparams("arbitrary"),
        name="pack_w",
    )(w_in, w_in, w_gate)


def _proj_kernel(x_ref, g_ref, w_ref, b_ref, o_ref, h_ref, *, acts):
    j = pl.program_id(1)

    @pl.when(j == 0)
    def _():
        h_ref[...] = _rms(x_ref[...], g_ref[...]).astype(BF16)

    for act in sorted(set(acts), key=lambda f: f.__name__):
        pred = functools.reduce(jnp.logical_or, [j == t for t, a in enumerate(acts) if a is act])

        @pl.when(pred)
        def _(act=act):
            acc = _dot(h_ref[...], w_ref[...]) + b_ref[...]
            o_ref[...] = act(acc).astype(o_ref.dtype)


def _projection(x2, g_pre, w_all, b_all, acts, tm, tn):
    seq, d = x2.shape
    n = w_all.shape[1]
    return pl.pallas_call(
        functools.partial(_proj_kernel, acts=acts),
        grid=(seq // tm, n // tn),
        in_specs=[
            pl.BlockSpec((tm, d), lambda i, j: (i, 0)),
            pl.BlockSpec((1, d), lambda i, j: (0, 0)),
            pl.BlockSpec((d, tn), lambda i, j: (0, j)),
            pl.BlockSpec((1, tn), lambda i, j: (0, j)),
        ],
        out_specs=[pl.BlockSpec((tm, tn), lambda i, j: (i, j)),
                   pl.BlockSpec((tm, d), lambda i, j: (i, 0))],
        out_shape=[jax.ShapeDtypeStruct((seq, n), BF16), jax.ShapeDtypeStruct((seq, d), BF16)],
        compiler_params=_cparams("parallel", "arbitrary"),
        name="in_proj",
    )(x2, g_pre, w_all, b_all)


def _branch_a_kernel(gu_ref, gv_ref, sz_ref, lng_ref, lnb_ref, ws_ref, bs_ref, o_ref, *, tm):
    gv = gv_ref[...].astype(F32)
    mu = jnp.mean(gv, axis=-1, keepdims=True)
    cen = gv - mu
    var = jnp.mean(cen * cen, axis=-1, keepdims=True)
    vn = (cen * lax.rsqrt(var + EPS) * lng_ref[...] + lnb_ref[...]).astype(BF16)
    row = lax.broadcasted_iota(jnp.int32, (CHUNK, CHUNK), 0)
    col = lax.broadcasted_iota(jnp.int32, (CHUNK, CHUNK), 1)
    causal = col <= row
    for g in range(A_GROUPS):
        ws = jnp.where(causal, ws_ref[g], 0.0).astype(BF16)
        cs = slice(g * CHUNK, (g + 1) * CHUNK)
        for c in range(tm // CHUNK):
            rs = slice(c * CHUNK, (c + 1) * CHUNK)
            sv = _dot(ws, vn[rs, cs]) + bs_ref[g]
            y = gu_ref[rs, cs].astype(F32) * sv * sz_ref[rs, cs].astype(F32)
            o_ref[rs, cs] = y.astype(o_ref.dtype)


def _branch_a(proj, ln_g, ln_b, w_s, b_s_b, offs, tm):
    seq = proj.shape[0]
    w = A_GROUPS * CHUNK
    blk = lambda off: pl.BlockSpec((tm, w), lambda i, off=off: (i, off // w))
    const = lambda shape: pl.BlockSpec(shape, lambda i: (0,) * len(shape))
    return pl.pallas_call(
        functools.partial(_branch_a_kernel, tm=tm),
        grid=(seq // tm,),
        in_specs=[blk(offs["u"]), blk(offs["v"]), blk(offs["za"]),
                  const((1, w)), const((1, w)),
                  const((A_GROUPS, CHUNK, CHUNK)), const((A_GROUPS, CHUNK, CHUNK))],
        out_specs=pl.BlockSpec((tm, w), lambda i: (i, 0)),
        out_shape=jax.ShapeDtypeStruct((seq, w), BF16),
        compiler_params=_cparams("parallel"),
        name="branch_a",
    )(proj, proj, proj, ln_g, ln_b, w_s, b_s_b)


def _mla_prep_kernel(cq_ref, ckv_ref, h_ref, wkr_ref, tc_ref, ta_ref, tb_ref, qg_ref, kvg_ref,
                     wq_ref, wkv_ref, q_ref, k_ref, vt_ref):
    scale = QK_DIM ** -0.5 * LOG2E
    tab_c, tab_a, tab_b = tc_ref[...], ta_ref[...], tb_ref[...]
    cqn = _rms(cq_ref[...].astype(F32), qg_ref[...]).astype(BF16)
    ckvn = _rms(ckv_ref[...].astype(F32), kvg_ref[...]).astype(BF16)
    kpe = _rope(_dot(h_ref[...], wkr_ref[...]), tab_c, tab_a, tab_b).astype(BF16)
    for h in range(MLA_HEADS):
        lo, mid, hi = h * HEAD_PAD, h * HEAD_PAD + LANE, (h + 1) * HEAD_PAD
        q = _dot(cqn, wq_ref[:, lo:hi]) * scale
        q_ref[:, lo:mid] = q[:, :LANE].astype(BF16)
        q_ref[:, mid:hi] = _rope(q[:, LANE:], tab_c, tab_a, tab_b).astype(BF16)
        kv = _dot(ckvn, wkv_ref[:, lo:hi])
        k_ref[:, lo:mid] = kv[:, :LANE].astype(BF16)
        k_ref[:, mid:hi] = kpe
        vt_ref[h * V_DIM:(h + 1) * V_DIM, :] = kv[:, LANE:].T.astype(BF16)


def _mla_prep(proj, h, w_kr, tabs, q_norm_g, kv_norm_g, wq_pad, wkv, offs, tm):
    seq, d = h.shape
    hw = MLA_HEADS * HEAD_PAD
    const = lambda shape: pl.BlockSpec(shape, lambda i: (0,) * len(shape))
    col = lambda off, w: pl.BlockSpec((tm, w), lambda i: (i, off // w))
    tab = pl.BlockSpec((tm, LANE), lambda i: (i, 0))
    return pl.pallas_call(
        _mla_prep_kernel,
        grid=(seq // tm,),
        in_specs=[col(offs["cq"], Q_LORA), col(offs["ckv"], KV_LORA),
                  pl.BlockSpec((tm, d), lambda i: (i, 0)), const((d, LANE)),
                  tab, tab, tab, const((1, Q_LORA)), const((1, KV_LORA)),
                  const((Q_LORA, hw)), const((KV_LORA, hw))],
        out_specs=[pl.BlockSpec((tm, hw), lambda i: (i, 0)),
                   pl.BlockSpec((tm, hw), lambda i: (i, 0)),
                   pl.BlockSpec((MLA_HEADS * V_DIM, tm), lambda i: (0, i))],
        out_shape=[jax.ShapeDtypeStruct((seq, hw), BF16),
                   jax.ShapeDtypeStruct((seq, hw), BF16),
                   jax.ShapeDtypeStruct((MLA_HEADS * V_DIM, seq), BF16)],
        compiler_params=_cparams("parallel"),
        name="mla_prep",
    )(proj, proj, h, w_kr, *tabs, q_norm_g, kv_norm_g, wq_pad, wkv)


def _flash_kernel(q_ref, k_ref, vt_ref, g_ref, o_ref, *scratch, tq, tk, cb):
    qi = pl.program_id(1)
    ncb = tq // cb
    take = lambda n: [scratch[i * ncb:(i + 1) * ncb] for i in range(n)]
    m_refs, l_refs, acc_refs, s0, s1, mn0, mn1, al0, al1 = take(9)
    s_refs, mnew_refs, alpha_refs = (s0, s1), (mn0, mn1), (al0, al1)
    for c in range(ncb):
        m_refs[c][...] = jnp.full(m_refs[c].shape, -1e30, F32)
        l_refs[c][...] = jnp.zeros(l_refs[c].shape, F32)
        acc_refs[c][...] = jnp.zeros(acc_refs[c].shape, F32)

    def scores(c, kstart, rows, slot, mask_off=None):
        s = _dot_nt(k_ref[pl.ds(kstart, rows), :], q_ref[c * cb:(c + 1) * cb, :])
        if mask_off is not None:
            key = lax.broadcasted_iota(jnp.int32, s.shape, 0)
            qry = lax.broadcasted_iota(jnp.int32, s.shape, 1)
            s = jnp.where(key <= qry + mask_off, s, -1e30)
        m_prev = m_refs[c][...]
        m_new = jnp.maximum(m_prev, jnp.max(s, axis=0, keepdims=True))
        s_refs[slot][c][:rows, :] = s
        mnew_refs[slot][c][...] = m_new
        alpha_refs[slot][c][...] = jnp.exp2(m_prev - m_new)
        m_refs[c][...] = m_new

    def update(c, kstart, rows, slot):
        alpha = alpha_refs[slot][c][...]
        p = jnp.exp2(s_refs[slot][c][:rows, :] - mnew_refs[slot][c][...])
        l_refs[c][...] = alpha * l_refs[c][...] + jnp.sum(p, axis=0, keepdims=True)
        pv = _dot(vt_ref[:, pl.ds(kstart, rows)], p.astype(BF16))
        acc_refs[c][...] = alpha * acc_refs[c][...] + pv

    def kv(t):
        return pl.multiple_of(t * tk, tk)

    assert tq == 2 * tk

    @pl.when(qi > 0)
    def _():
        for c in range(ncb):
            scores(c, 0, tk, 0)

        def body(i, carry):
            for c in range(ncb):
                scores(c, kv(2 * i + 1), tk, 1)
                update(c, kv(2 * i), tk, 0)
            for c in range(ncb):
                scores(c, kv(2 * i + 2), tk, 0)
                update(c, kv(2 * i + 1), tk, 1)
            return carry

        lax.fori_loop(0, qi - 1, body, 0)
        for c in range(ncb):
            scores(c, kv(2 * qi - 1), tk, 1)
            update(c, kv(2 * qi - 2), tk, 0)

    @pl.when(qi == 0)
    def _():
        for c in range(ncb):
            s_refs[1][c][...] = jnp.zeros(s_refs[1][c].shape, F32)
            mnew_refs[1][c][...] = jnp.full(mnew_refs[1][c].shape, 1e30, F32)
            alpha_refs[1][c][...] = jnp.ones(alpha_refs[1][c].shape, F32)

    pending = kv(jnp.maximum(2 * qi - 1, 0))
    base = pl.multiple_of(qi * tq, tq)
    diag = [[], []]
    for d in range(2):
        for c in range(ncb):
            rows = min(tk, (c + 1) * cb - d * tk)
            if rows > 0:
                crosses = d * tk + rows - 1 > c * cb
                diag[d].append(((c, pl.multiple_of(base + d * tk, tk), rows, d),
                                c * cb - d * tk if crosses else None))
    assert [item[0] for item, _ in diag[0]] == list(range(ncb))
    for item, mask_off in diag[0]:
        scores(*item, mask_off)
        update(item[0], pending, tk, 1)
    for n, (item, mask_off) in enumerate(diag[1]):
        scores(*item, mask_off)
        update(*diag[0][n][0])
    for item, _ in diag[0][len(diag[1]):] + diag[1]:
        update(*item)

    for c in range(ncb):
        o = (acc_refs[c][...] / l_refs[c][...]).T
        rs = slice(c * cb, (c + 1) * cb)
        o_ref[rs, :] = (o * g_ref[rs, :].astype(F32)).astype(o_ref.dtype)


def _flash(q, k, vt, proj, zb_off, tq, tk, cb):
    seq = q.shape[0]
    ncb = tq // cb
    stat = [pltpu.VMEM((1, cb), F32)] * ncb
    scratch = (stat + stat + [pltpu.VMEM((V_DIM, cb), F32)] * ncb
               + [pltpu.VMEM((tk, cb), F32)] * (2 * ncb)
               + stat * 4)
    return pl.pallas_call(
        functools.partial(_flash_kernel, tq=tq, tk=tk, cb=cb),
        grid=(MLA_HEADS, seq // tq),
        in_specs=[pl.BlockSpec((tq, HEAD_PAD), lambda h, i: (i, h)),
                  pl.BlockSpec((seq, HEAD_PAD), lambda h, i: (0, h)),
                  pl.BlockSpec((V_DIM, seq), lambda h, i: (h, 0)),
                  pl.BlockSpec((tq, V_DIM), lambda h, i: (i, zb_off // V_DIM + h))],
        out_specs=pl.BlockSpec((tq, V_DIM), lambda h, i: (i, h)),
        out_shape=jax.ShapeDtypeStruct((seq, MLA_HEADS * V_DIM), BF16),
        scratch_shapes=scratch,
        compiler_params=_cparams("parallel", "arbitrary"),
        name="flash_mla",
    )(q, k, vt, proj)


def _mem_kv_kernel(mem_ref, g_ref, w_ref, o_ref):
    mn = _rms(mem_ref[...], g_ref[...]).astype(BF16)
    o_ref[...] = _dot(mn, w_ref[...].astype(BF16)).astype(o_ref.dtype)


def _mem_kv(mem2, mem_norm_g, w_mem_kv, tn):
    m, d = mem2.shape
    n = w_mem_kv.shape[1]
    return pl.pallas_call(
        _mem_kv_kernel,
        grid=(n // tn,),
        in_specs=[pl.BlockSpec((m, d), lambda j: (0, 0)),
                  pl.BlockSpec((1, d), lambda j: (0, 0)),
                  pl.BlockSpec((d, tn), lambda j: (0, j))],
        out_specs=pl.BlockSpec((m, tn), lambda j: (0, j)),
        out_shape=jax.ShapeDtypeStruct((m, n), BF16),
        compiler_params=_cparams("parallel"),
        name="mem_kv",
    )(mem2, mem_norm_g, w_mem_kv)


def _mem_attn_kernel(q_ref, kv_ref, sz_ref, o_ref, *, width):
    hd = width // MEM_HEADS
    scale = hd ** -0.5
    for h in range(MEM_HEADS):
        cs = slice(h * hd, (h + 1) * hd)
        s = _dot_nt(q_ref[:, cs], kv_ref[:, cs]) * scale
        p = jnp.exp(s - jnp.max(s, axis=-1, keepdims=True))
        l = jnp.sum(p, axis=-1, keepdims=True)
        o = _dot(p.astype(BF16), kv_ref[:, width + h * hd:width + (h + 1) * hd]) / l
        o_ref[:, cs] = (o * sz_ref[:, cs].astype(F32)).astype(o_ref.dtype)


def _mem_attn(proj, kvm, offs, width, tm):
    seq = proj.shape[0]
    blk = lambda off: pl.BlockSpec((tm, width), lambda i: (i, off // width))
    return pl.pallas_call(
        functools.partial(_mem_attn_kernel, width=width),
        grid=(seq // tm,),
        in_specs=[blk(offs["qm"]), pl.BlockSpec(kvm.shape, lambda i: (0, 0)), blk(offs["zm"])],
        out_specs=pl.BlockSpec((tm, width), lambda i: (i, 0)),
        out_shape=jax.ShapeDtypeStruct((seq, width), BF16),
        compiler_params=_cparams("parallel"),
        name="mem_attn",
    )(proj, kvm, proj)


def _merge_kernel(ya_ref, yb_ref, ym_ref, w_ref, g0_ref, g1_ref, g2_ref, o_ref):
    acc = g0_ref[...].astype(F32) * _dot(ya_ref[...], w_ref[0])
    acc += g1_ref[...].astype(F32) * _dot(yb_ref[...], w_ref[1])
    acc += g2_ref[...].astype(F32) * _dot(ym_ref[...], w_ref[2])
    o_ref[...] = acc.astype(o_ref.dtype)


def _merge(y_a, y_b, y_m, w_branch, proj, gate_off, tm, tn):
    seq, width = y_a.shape
    d = w_branch.shape[2]
    yblk = pl.BlockSpec((tm, width), lambda i, j: (i, 0))
    gblk = lambda n: pl.BlockSpec((tm, tn), lambda i, j, n=n: (i, (gate_off + n * d) // tn + j))
    return pl.pallas_call(
        _merge_kernel,
        grid=(seq // tm, d // tn),
        in_specs=[yblk, yblk, yblk,
                  pl.BlockSpec((N_BRANCH, width, tn), lambda i, j: (0, 0, j)),
                  gblk(0), gblk(1), gblk(2)],
        out_specs=pl.BlockSpec((tm, tn), lambda i, j: (i, j)),
        out_shape=jax.ShapeDtypeStruct((seq, d), BF16),
        compiler_params=_cparams("parallel", "arbitrary"),
        name="merge",
    )(y_a, y_b, y_m, w_branch, proj, proj, proj)


def _out_kernel(m_ref, w_ref, x_ref, g_ref, o_ref):
    out = _dot(m_ref[...], w_ref[...])
    o_ref[...] = x_ref[...] + _rms(out, g_ref[...])


def _out_proj(merged, w_out, x2, g_post, tm):
    seq, d = x2.shape
    return pl.pallas_call(
        _out_kernel,
        grid=(seq // tm,),
        in_specs=[pl.BlockSpec((tm, d), lambda i: (i, 0)),
                  pl.BlockSpec((d, d), lambda i: (0, 0)),
                  pl.BlockSpec((tm, d), lambda i: (i, 0)),
                  pl.BlockSpec((1, d), lambda i: (0, 0))],
        out_specs=pl.BlockSpec((tm, d), lambda i: (i, 0)),
        out_shape=jax.ShapeDtypeStruct((seq, d), F32),
        compiler_params=_cparams("parallel"),
        name="out_proj",
    )(merged, w_out, x2, g_post)


def _layer(x2, mem2, tabs, g_pre, w_in, a_ln_g, a_ln_b, a_w_s, a_b_s, q_norm_g, w_uq,
           kv_norm_g, w_ukv, mem_norm_g, w_mem_kv, w_gate, b_gate, w_branch, w_out, g_post):
    seq, d = x2.shape
    tn = 1024
    w_all, w_kr = _pack_weights(w_in, w_gate, tn=512)
    offs = {"u": 0, "v": d, "za": 2 * d, "zb": 3 * d, "qm": 4 * d, "zm": 5 * d, "gate": 6 * d,
            "cq": 9 * d, "ckv": 9 * d + Q_LORA}
    n_all = w_all.shape[1]
    b_all = jnp.zeros((1, n_all), F32).at[0, offs["gate"]:offs["gate"] + N_BRANCH * d].set(b_gate)
    sections = (("u", _gelu), ("v", _gelu), ("za", _silu), ("zb", _silu), ("qm", _identity),
                ("zm", _silu), ("gate", _sigmoid), ("cq", _identity))
    starts = [offs[name] // tn for name, _ in sections] + [n_all // tn]
    acts = tuple(act for (_, act), lo, hi in zip(sections, starts[:-1], starts[1:])
                 for _ in range(hi - lo))

    proj, h = _projection(x2, g_pre.reshape(1, d), w_all, b_all, acts, tm=1024, tn=tn)

    b_s_b = jnp.broadcast_to(a_b_s[:, :, None], (A_GROUPS, CHUNK, CHUNK))
    y_a = _branch_a(proj, a_ln_g.reshape(1, d), a_ln_b.reshape(1, d), a_w_s, b_s_b, offs, tm=256)

    wq_pad = jnp.pad(w_uq.reshape(Q_LORA, MLA_HEADS, QK_DIM),
                     ((0, 0), (0, 0), (0, HEAD_PAD - QK_DIM))).reshape(Q_LORA, -1).astype(BF16)
    q, k, vv = _mla_prep(proj, h, w_kr, tabs, q_norm_g.reshape(1, -1), kv_norm_g.reshape(1, -1),
                         wq_pad, w_ukv.astype(BF16), offs, tm=512)
    y_b = _flash(q, k, vv, proj, offs["zb"], tq=1024, tk=512, cb=256)

    kvm = _mem_kv(mem2, mem_norm_g.reshape(1, d), w_mem_kv, tn=512)
    y_m = _mem_attn(proj, kvm, offs, d, tm=512)

    merged = _merge(y_a, y_b, y_m, w_branch.astype(BF16), proj, offs["gate"], tm=1024, tn=512)
    return _out_proj(merged, w_out.astype(BF16), x2, g_post.reshape(1, d), tm=512)


def kernel(x, mem, positions, g_pre, w_in, a_ln_g, a_ln_b, a_w_s, a_b_s, q_norm_g, w_uq,
           kv_norm_g, w_ukv, mem_norm_g, w_mem_kv, w_gate, b_gate, w_branch, w_out, g_post):
    batch, seq, d = x.shape
    depth = g_pre.shape[0]
    outs = []
    for b in range(batch):
        tabs = _rope_tables(positions[b], seq)
        xb = x[b]
        for l in range(depth):
            xb = _layer(xb, mem[b], tabs, g_pre[l], w_in[l], a_ln_g[l], a_ln_b[l], a_w_s[l],
                        a_b_s[l], q_norm_g[l], w_uq[l], kv_norm_g[l], w_ukv[l], mem_norm_g[l],
                        w_mem_kv[l], w_gate[l], b_gate[l], w_branch[l], w_out[l], g_post[l])
        outs.append(xb)
    return jnp.stack(outs, axis=0)
```

```python
import functools

import jax
import jax.numpy as jnp
from jax import lax
from jax.experimental import pallas as pl
from jax.experimental.pallas import tpu as pltpu

EPS = 1e-6
CHUNK = 128
A_GROUPS = 16
QK_NOPE = 128
QK_ROPE = 64
V_DIM = 128
MLA_HEADS = 16
Q_LORA = 512
KV_LORA = 512
QK_DIM = QK_NOPE + QK_ROPE
ROPE_THETA = 10000.0
LOG2E = 1.4426950408889634
GELU_C = 0.7978845608028654
PROJ_CHUNK = 256
MEM_HEADS = 4
N_BRANCH = 3
LANE = 128
HEAD_PAD = 256
VMEM_LIMIT = 52 * 1024 * 1024

F32 = jnp.float32
BF16 = jnp.bfloat16


def _cparams(*sem):
    return pltpu.CompilerParams(dimension_semantics=sem, vmem_limit_bytes=VMEM_LIMIT)


def _dot(a, b):
    return jnp.dot(a, b, preferred_element_type=F32)


def _dot_nt(a, b):
    return lax.dot_general(a, b, (((1,), (1,)), ((), ())), preferred_element_type=F32)


def _gelu(x):
    half = 0.5 * x
    return half + half * jnp.tanh(x * (GELU_C + (GELU_C * 0.044715) * (x * x)))


def _sigmoid(x):
    return 0.5 + 0.5 * jnp.tanh(0.5 * x)


def _silu(x):
    half = 0.5 * x
    return half + half * jnp.tanh(half)


def _identity(x):
    return x


def _rms(xf, g):
    return xf * lax.rsqrt(jnp.mean(xf * xf, axis=-1, keepdims=True) + EPS) * g


def _rope_table_kernel(pos_ref, invf_ref, cos_ref, sin_ref):
    ang = pos_ref[...].astype(F32) * invf_ref[...]
    cos_ref[...] = jnp.cos(ang)
    sin_ref[...] = jnp.sin(ang)


def _rope_tables(positions, seq):
    half = QK_ROPE // 2
    inv_freq = 1.0 / (ROPE_THETA ** (jnp.arange(0, QK_ROPE, 2, dtype=F32) / QK_ROPE))
    cos_t, sin_t = pl.pallas_call(
        _rope_table_kernel,
        out_shape=(jax.ShapeDtypeStruct((half, seq), F32),) * 2,
        name="rope_tables",
    )(positions.reshape(1, seq), inv_freq.reshape(half, 1))
    cos, sin = cos_t.T, sin_t.T
    z32 = jnp.zeros((seq, half), F32)
    z64 = jnp.zeros((seq, 2 * half), F32)
    tab_c = jnp.concatenate([cos, cos, z64], axis=-1)
    tab_a = jnp.concatenate([sin, z32, z64], axis=-1)
    tab_b = jnp.concatenate([z32, sin, z64], axis=-1)
    return tab_c, tab_a, tab_b


def _rope(t, tab_c, tab_a, tab_b):
    down = pltpu.roll(t, 3 * LANE // 4, axis=1)
    up = pltpu.roll(t, LANE // 4, axis=1)
    return t * tab_c - down * tab_a + up * tab_b


def _proj_kernel(x_ref, g_ref, wt_ref, wg_ref, b_ref, o_ref, h_ref, *, tiles):
    j = pl.program_id(1)

    @pl.when(j == 0)
    def _():
        h_ref[...] = _rms(x_ref[...], g_ref[...]).astype(BF16)

    kinds = sorted({(act, row is None) for act, row in tiles}, key=lambda k: (k[0].__name__, k[1]))
    for act, is_gate in kinds:
        pred = functools.reduce(jnp.logical_or, [j == t for t, (a, row) in enumerate(tiles)
                                                 if a is act and (row is None) == is_gate])

        @pl.when(pred)
        def _(act=act, is_gate=is_gate):
            for lo in range(0, o_ref.shape[1], PROJ_CHUNK):
                cs = slice(lo, lo + PROJ_CHUNK)
                if is_gate:
                    acc = _dot(h_ref[...], wg_ref[:, cs]) + b_ref[:, cs]
                else:
                    acc = _dot_nt(h_ref[...], wt_ref[cs, :])
                o_ref[:, cs] = act(acc).astype(o_ref.dtype)


def _projection(x2, g_pre, w_in_t, w_gate, b_gate, tiles, tm, tn):
    seq, d = x2.shape
    n_tiles = len(tiles)

    def pick(j, values):
        return functools.reduce(lambda acc, tv: jnp.where(j == tv[0], tv[1], acc),
                                enumerate(values), values[0])

    rows, gates, row, gate = [], [], 0, 0
    for act, src in tiles:
        if src is None:
            gates.append(gate)
            gate += 1
        else:
            row = src
            gates.append(max(gate - 1, 0))
        rows.append(row)
    return pl.pallas_call(
        functools.partial(_proj_kernel, tiles=tiles),
        grid=(seq // tm, n_tiles),
        in_specs=[
            pl.BlockSpec((tm, d), lambda i, j: (i, 0)),
            pl.BlockSpec((1, d), lambda i, j: (0, 0)),
            pl.BlockSpec((pl.Element(tn), pl.Element(d)),
                         lambda i, j: (pick(j, [r // QK_ROPE for r in rows]) * QK_ROPE, 0)),
            pl.BlockSpec((d, tn), lambda i, j: (0, pick(j, gates))),
            pl.BlockSpec((1, tn), lambda i, j: (0, pick(j, gates))),
        ],
        out_specs=[pl.BlockSpec((tm, tn), lambda i, j: (i, j)),
                   pl.BlockSpec((tm, d), lambda i, j: (i, 0))],
        out_shape=[jax.ShapeDtypeStruct((seq, n_tiles * tn), BF16),
                   jax.ShapeDtypeStruct((seq, d), BF16)],
        compiler_params=_cparams("parallel", "arbitrary"),
        name="in_proj",
    )(x2, g_pre, w_in_t, w_gate, b_gate)


def _branch_a_kernel(gu_ref, gv_ref, sz_ref, lng_ref, lnb_ref, ws_ref, bs_ref, o_ref, *, tm):
    gv = gv_ref[...].astype(F32)
    mu = jnp.mean(gv, axis=-1, keepdims=True)
    cen = gv - mu
    var = jnp.mean(cen * cen, axis=-1, keepdims=True)
    vn = (cen * lax.rsqrt(var + EPS) * lng_ref[...] + lnb_ref[...]).astype(BF16)
    row = lax.broadcasted_iota(jnp.int32, (CHUNK, CHUNK), 0)
    col = lax.broadcasted_iota(jnp.int32, (CHUNK, CHUNK), 1)
    causal = col <= row
    for g in range(A_GROUPS):
        ws = jnp.where(causal, ws_ref[g], 0.0).astype(BF16)
        cs = slice(g * CHUNK, (g + 1) * CHUNK)
        for c in range(tm // CHUNK):
            rs = slice(c * CHUNK, (c + 1) * CHUNK)
            sv = _dot(ws, vn[rs, cs]) + bs_ref[g]
            y = gu_ref[rs, cs].astype(F32) * sv * sz_ref[rs, cs].astype(F32)
            o_ref[rs, cs] = y.astype(o_ref.dtype)


def _branch_a(proj, ln_g, ln_b, w_s, b_s_b, offs, tm):
    seq = proj.shape[0]
    w = A_GROUPS * CHUNK
    blk = lambda off: pl.BlockSpec((tm, w), lambda i, off=off: (i, off // w))
    const = lambda shape: pl.BlockSpec(shape, lambda i: (0,) * len(shape))
    return pl.pallas_call(
        functools.partial(_branch_a_kernel, tm=tm),
        grid=(seq // tm,),
        in_specs=[blk(offs["u"]), blk(offs["v"]), blk(offs["za"]),
                  const((1, w)), const((1, w)),
                  const((A_GROUPS, CHUNK, CHUNK)), const((A_GROUPS, CHUNK, CHUNK))],
        out_specs=pl.BlockSpec((tm, w), lambda i: (i, 0)),
        out_shape=jax.ShapeDtypeStruct((seq, w), BF16),
        compiler_params=_cparams("parallel"),
        name="branch_a",
    )(proj, proj, proj, ln_g, ln_b, w_s, b_s_b)


def _mla_prep_kernel(cq_ref, ckv_ref, h_ref, wkr_ref, tc_ref, ta_ref, tb_ref, qg_ref, kvg_ref,
                     wq_ref, wkv_ref, q_ref, k_ref, vt_ref):
    scale = QK_DIM ** -0.5 * LOG2E
    tab_c, tab_a, tab_b = tc_ref[...], ta_ref[...], tb_ref[...]
    cqn = _rms(cq_ref[...].astype(F32), qg_ref[...]).astype(BF16)
    ckvn = _rms(ckv_ref[...].astype(F32), kvg_ref[...]).astype(BF16)
    kpe = _rope(_dot_nt(h_ref[...], wkr_ref[...]), tab_c, tab_a, tab_b).astype(BF16)
    for h in range(MLA_HEADS):
        lo, mid, hi = h * HEAD_PAD, h * HEAD_PAD + LANE, (h + 1) * HEAD_PAD
        q = _dot(cqn, wq_ref[:, lo:hi]) * scale
        q_ref[:, lo:mid] = q[:, :LANE].astype(BF16)
        q_ref[:, mid:hi] = _rope(q[:, LANE:], tab_c, tab_a, tab_b).astype(BF16)
        kv = _dot(ckvn, wkv_ref[:, lo:hi])
        k_ref[:, lo:mid] = kv[:, :LANE].astype(BF16)
        k_ref[:, mid:hi] = kpe
        vt_ref[h * V_DIM:(h + 1) * V_DIM, :] = kv[:, LANE:].T.astype(BF16)


def _mla_prep(proj, h, w_kr, tabs, q_norm_g, kv_norm_g, wq_pad, wkv, offs, tm):
    seq, d = h.shape
    hw = MLA_HEADS * HEAD_PAD
    const = lambda shape: pl.BlockSpec(shape, lambda i: (0,) * len(shape))
    col = lambda off, w: pl.BlockSpec((tm, w), lambda i: (i, off // w))
    tab = pl.BlockSpec((tm, LANE), lambda i: (i, 0))
    return pl.pallas_call(
        _mla_prep_kernel,
        grid=(seq // tm,),
        in_specs=[col(offs["cq"], Q_LORA), col(offs["ckv"], KV_LORA),
                  pl.BlockSpec((tm, d), lambda i: (i, 0)), const((LANE, d)),
                  tab, tab, tab, const((1, Q_LORA)), const((1, KV_LORA)),
                  const((Q_LORA, hw)), const((KV_LORA, hw))],
        out_specs=[pl.BlockSpec((tm, hw), lambda i: (i, 0)),
                   pl.BlockSpec((tm, hw), lambda i: (i, 0)),
                   pl.BlockSpec((MLA_HEADS * V_DIM, tm), lambda i: (0, i))],
        out_shape=[jax.ShapeDtypeStruct((seq, hw), BF16),
                   jax.ShapeDtypeStruct((seq, hw), BF16),
                   jax.ShapeDtypeStruct((MLA_HEADS * V_DIM, seq), BF16)],
        compiler_params=_cparams("parallel"),
        name="mla_prep",
    )(proj, proj, h, w_kr, *tabs, q_norm_g, kv_norm_g, wq_pad, wkv)


def _flash_kernel(q_ref, k_ref, vt_ref, g_ref, o_ref, *scratch, tq, tk, cb):
    qi = pl.program_id(1)
    ncb = tq // cb
    take = lambda n: [scratch[i * ncb:(i + 1) * ncb] for i in range(n)]
    m_refs, l_refs, acc_refs, s0, s1, mn0, mn1, al0, al1 = take(9)
    s_refs, mnew_refs, alpha_refs = (s0, s1), (mn0, mn1), (al0, al1)
    for c in range(ncb):
        m_refs[c][...] = jnp.full(m_refs[c].shape, -1e30, F32)
        l_refs[c][...] = jnp.zeros(l_refs[c].shape, F32)
        acc_refs[c][...] = jnp.zeros(acc_refs[c].shape, F32)

    def scores(c, kstart, rows, slot, mask_off=None):
        s = _dot_nt(k_ref[pl.ds(kstart, rows), :], q_ref[c * cb:(c + 1) * cb, :])
        if mask_off is not None:
            key = lax.broadcasted_iota(jnp.int32, s.shape, 0)
            qry = lax.broadcasted_iota(jnp.int32, s.shape, 1)
            s = jnp.where(key <= qry + mask_off, s, -1e30)
        m_prev = m_refs[c][...]
        m_new = jnp.maximum(m_prev, jnp.max(s, axis=0, keepdims=True))
        s_refs[slot][c][:rows, :] = s
        mnew_refs[slot][c][...] = m_new
        alpha_refs[slot][c][...] = jnp.exp2(m_prev - m_new)
        m_refs[c][...] = m_new

    def update(c, kstart, rows, slot):
        alpha = alpha_refs[slot][c][...]
        p = jnp.exp2(s_refs[slot][c][:rows, :] - mnew_refs[slot][c][...])
        l_refs[c][...] = alpha * l_refs[c][...] + jnp.sum(p, axis=0, keepdims=True)
        pv = _dot(vt_ref[:, pl.ds(kstart, rows)], p.astype(BF16))
        acc_refs[c][...] = alpha * acc_refs[c][...] + pv

    def kv(t):
        return pl.multiple_of(t * tk, tk)

    assert tq == 2 * tk

    @pl.when(qi > 0)
    def _():
        for c in range(ncb):
            scores(c, 0, tk, 0)

        def body(i, carry):
            for c in range(ncb):
                scores(c, kv(2 * i + 1), tk, 1)
                update(c, kv(2 * i), tk, 0)
            for c in range(ncb):
                scores(c, kv(2 * i + 2), tk, 0)
                update(c, kv(2 * i + 1), tk, 1)
            return carry

        lax.fori_loop(0, qi - 1, body, 0)
        for c in range(ncb):
            scores(c, kv(2 * qi - 1), tk, 1)
            update(c, kv(2 * qi - 2), tk, 0)

    @pl.when(qi == 0)
    def _():
        for c in range(ncb):
            s_refs[1][c][...] = jnp.zeros(s_refs[1][c].shape, F32)
            mnew_refs[1][c][...] = jnp.full(mnew_refs[1][c].shape, 1e30, F32)
            alpha_refs[1][c][...] = jnp.ones(alpha_refs[1][c].shape, F32)

    pending = kv(jnp.maximum(2 * qi - 1, 0))
    base = pl.multiple_of(qi * tq, tq)
    diag = [[], []]
    for d in range(2):
        for c in range(ncb):
            rows = min(tk, (c + 1) * cb - d * tk)
            if rows > 0:
                crosses = d * tk + rows - 1 > c * cb
                diag[d].append(((c, pl.multiple_of(base + d * tk, tk), rows, d),
                                c * cb - d * tk if crosses else None))
    assert [item[0] for item, _ in diag[0]] == list(range(ncb))
    for item, mask_off in diag[0]:
        scores(*item, mask_off)
        update(item[0], pending, tk, 1)
    for n, (item, mask_off) in enumerate(diag[1]):
        scores(*item, mask_off)
        update(*diag[0][n][0])
    for item, _ in diag[0][len(diag[1]):] + diag[1]:
        update(*item)

    for c in range(ncb):
        o = (acc_refs[c][...] / l_refs[c][...]).T
        rs = slice(c * cb, (c + 1) * cb)
        o_ref[rs, :] = (o * g_ref[rs, :].astype(F32)).astype(o_ref.dtype)


def _flash(q, k, vt, proj, zb_off, tq, tk, cb):
    seq = q.shape[0]
    ncb = tq // cb
    stat = [pltpu.VMEM((1, cb), F32)] * ncb
    scratch = (stat + stat + [pltpu.VMEM((V_DIM, cb), F32)] * ncb
               + [pltpu.VMEM((tk, cb), F32)] * (2 * ncb)
               + stat * 4)
    return pl.pallas_call(
        functools.partial(_flash_kernel, tq=tq, tk=tk, cb=cb),
        grid=(MLA_HEADS, seq // tq),
        in_specs=[pl.BlockSpec((tq, HEAD_PAD), lambda h, i: (i, h)),
                  pl.BlockSpec((seq, HEAD_PAD), lambda h, i: (0, h)),
                  pl.BlockSpec((V_DIM, seq), lambda h, i: (h, 0)),
                  pl.BlockSpec((tq, V_DIM), lambda h, i: (i, zb_off // V_DIM + h))],
        out_specs=pl.BlockSpec((tq, V_DIM), lambda h, i: (i, h)),
        out_shape=jax.ShapeDtypeStruct((seq, MLA_HEADS * V_DIM), BF16),
        scratch_shapes=scratch,
        compiler_params=_cparams("parallel", "arbitrary"),
        name="flash_mla",
    )(q, k, vt, proj)


def _mem_kv_kernel(mem_ref, g_ref, w_ref, o_ref):
    mn = _rms(mem_ref[...], g_ref[...]).astype(BF16)
    o_ref[...] = _dot(mn, w_ref[...].astype(BF16)).astype(o_ref.dtype)


def _mem_kv(mem2, mem_norm_g, w_mem_kv, tn):
    m, d = mem2.shape
    n = w_mem_kv.shape[1]
    return pl.pallas_call(
        _mem_kv_kernel,
        grid=(n // tn,),
        in_specs=[pl.BlockSpec((m, d), lambda j: (0, 0)),
                  pl.BlockSpec((1, d), lambda j: (0, 0)),
                  pl.BlockSpec((d, tn), lambda j: (0, j))],
        out_specs=pl.BlockSpec((m, tn), lambda j: (0, j)),
        out_shape=jax.ShapeDtypeStruct((m, n), BF16),
        compiler_params=_cparams("parallel"),
        name="mem_kv",
    )(mem2, mem_norm_g, w_mem_kv)


def _mem_attn_kernel(q_ref, kv_ref, sz_ref, o_ref, *, width):
    hd = width // MEM_HEADS
    scale = hd ** -0.5
    for h in range(MEM_HEADS):
        cs = slice(h * hd, (h + 1) * hd)
        s = _dot_nt(q_ref[:, cs], kv_ref[:, cs]) * scale
        p = jnp.exp(s - jnp.max(s, axis=-1, keepdims=True))
        l = jnp.sum(p, axis=-1, keepdims=True)
        o = _dot(p.astype(BF16), kv_ref[:, width + h * hd:width + (h + 1) * hd]) / l
        o_ref[:, cs] = (o * sz_ref[:, cs].astype(F32)).astype(o_ref.dtype)


def _mem_attn(proj, kvm, offs, width, tm):
    seq = proj.shape[0]
    blk = lambda off: pl.BlockSpec((tm, width), lambda i: (i, off // width))
    return pl.pallas_call(
        functools.partial(_mem_attn_kernel, width=width),
        grid=(seq // tm,),
        in_specs=[blk(offs["qm"]), pl.BlockSpec(kvm.shape, lambda i: (0, 0)), blk(offs["zm"])],
        out_specs=pl.BlockSpec((tm, width), lambda i: (i, 0)),
        out_shape=jax.ShapeDtypeStruct((seq, width), BF16),
        compiler_params=_cparams("parallel"),
        name="mem_attn",
    )(proj, kvm, proj)


def _merge_kernel(ya_ref, yb_ref, ym_ref, w_ref, g0_ref, g1_ref, g2_ref, o_ref):
    acc = g0_ref[...].astype(F32) * _dot(ya_ref[...], w_ref[0])
    acc += g1_ref[...].astype(F32) * _dot(yb_ref[...], w_ref[1])
    acc += g2_ref[...].astype(F32) * _dot(ym_ref[...], w_ref[2])
    o_ref[...] = acc.astype(o_ref.dtype)


def _merge(y_a, y_b, y_m, w_branch, proj, gate_off, tm, tn):
    seq, width = y_a.shape
    d = w_branch.shape[2]
    yblk = pl.BlockSpec((tm, width), lambda i, j: (i, 0))
    gblk = lambda n: pl.BlockSpec((tm, tn), lambda i, j, n=n: (i, (gate_off + n * d) // tn + j))
    return pl.pallas_call(
        _merge_kernel,
        grid=(seq // tm, d // tn),
        in_specs=[yblk, yblk, yblk,
                  pl.BlockSpec((N_BRANCH, width, tn), lambda i, j: (0, 0, j)),
                  gblk(0), gblk(1), gblk(2)],
        out_specs=pl.BlockSpec((tm, tn), lambda i, j: (i, j)),
        out_shape=jax.ShapeDtypeStruct((seq, d), BF16),
        compiler_params=_cparams("parallel", "arbitrary"),
        name="merge",
    )(y_a, y_b, y_m, w_branch, proj, proj, proj)


def _out_kernel(m_ref, w_ref, x_ref, g_ref, o_ref):
    out = _dot(m_ref[...], w_ref[...])
    o_ref[...] = x_ref[...] + _rms(out, g_ref[...])


def _out_proj(merged, w_out, x2, g_post, tm):
    seq, d = x2.shape
    return pl.pallas_call(
        _out_kernel,
        grid=(seq // tm,),
        in_specs=[pl.BlockSpec((tm, d), lambda i: (i, 0)),
                  pl.BlockSpec((d, d), lambda i: (0, 0)),
                  pl.BlockSpec((tm, d), lambda i: (i, 0)),
                  pl.BlockSpec((1, d), lambda i: (0, 0))],
        out_specs=pl.BlockSpec((tm, d), lambda i: (i, 0)),
        out_shape=jax.ShapeDtypeStruct((seq, d), F32),
        compiler_params=_cparams("parallel"),
        name="out_proj",
    )(merged, w_out, x2, g_post)


def _layer(x2, mem2, tabs, g_pre, w_in, a_ln_g, a_ln_b, a_w_s, a_b_s, q_norm_g, w_uq,
           kv_norm_g, w_ukv, mem_norm_g, w_mem_kv, w_gate, b_gate, w_branch, w_out, g_post):
    seq, d = x2.shape
    tn = 1024
    w_in_t = jnp.swapaxes(w_in, 0, 1).astype(BF16)
    lat, tail = 3 * d, 3 * d + Q_LORA + KV_LORA
    w_kr = jnp.pad(w_in_t[tail:tail + QK_ROPE], ((0, LANE - QK_ROPE), (0, 0)))
    sections = (("u", _gelu, 0, d), ("v", _gelu, d, d), ("za", _silu, 2 * d, d),
                ("zb", _silu, tail + QK_ROPE, d), ("qm", _identity, tail + QK_ROPE + d, d),
                ("zm", _silu, tail + QK_ROPE + 2 * d, d), ("gate", _sigmoid, None, N_BRANCH * d),
                ("cq", _identity, lat, Q_LORA + KV_LORA))
    offs, tiles, col = {}, [], 0
    for name, act, row, width in sections:
        offs[name] = col
        tiles += [(act, None if row is None else row + t) for t in range(0, width, tn)]
        col += width
    offs["ckv"] = offs["cq"] + Q_LORA

    proj, h = _projection(x2, g_pre.reshape(1, d), w_in_t, w_gate.astype(BF16),
                          b_gate.reshape(1, -1), tuple(tiles), tm=1024, tn=tn)

    b_s_b = jnp.broadcast_to(a_b_s[:, :, None], (A_GROUPS, CHUNK, CHUNK))
    y_a = _branch_a(proj, a_ln_g.reshape(1, d), a_ln_b.reshape(1, d), a_w_s, b_s_b, offs, tm=256)

    wq_pad = jnp.pad(w_uq.reshape(Q_LORA, MLA_HEADS, QK_DIM),
                     ((0, 0), (0, 0), (0, HEAD_PAD - QK_DIM))).reshape(Q_LORA, -1).astype(BF16)
    q, k, vv = _mla_prep(proj, h, w_kr, tabs, q_norm_g.reshape(1, -1), kv_norm_g.reshape(1, -1),
                         wq_pad, w_ukv.astype(BF16), offs, tm=512)
    y_b = _flash(q, k, vv, proj, offs["zb"], tq=1024, tk=512, cb=256)

    kvm = _mem_kv(mem2, mem_norm_g.reshape(1, d), w_mem_kv, tn=512)
    y_m = _mem_attn(proj, kvm, offs, d, tm=512)

    merged = _merge(y_a, y_b, y_m, w_branch.astype(BF16), proj, offs["gate"], tm=1024, tn=512)
    return _out_proj(merged, w_out.astype(BF16), x2, g_post.reshape(1, d), tm=512)


def kernel(x, mem, positions, g_pre, w_in, a_ln_g, a_ln_b, a_w_s, a_b_s, q_norm_g, w_uq,
           kv_norm_g, w_ukv, mem_norm_g, w_mem_kv, w_gate, b_gate, w_branch, w_out, g_post):
    batch, seq, d = x.shape
    depth = g_pre.shape[0]
    outs = []
    for b in range(batch):
        tabs = _rope_tables(positions[b], seq)
        xb = x[b]
        for l in range(depth):
            xb = _layer(xb, mem[b], tabs, g_pre[l], w_in[l], a_ln_g[l], a_ln_b[l], a_w_s[l],
                        a_b_s[l], q_norm_g[l], w_uq[l], kv_norm_g[l], w_ukv[l], mem_norm_g[l],
                        w_mem_kv[l], w_gate[l], b_gate[l], w_branch[l], w_out[l], g_post[l])
        outs.append(xb)
    return jnp.stack(outs, axis=0)
```

```python
import functools

import jax
import jax.numpy as jnp
from jax import lax
from jax.experimental import pallas as pl
from jax.experimental.pallas import tpu as pltpu

EPS = 1e-6
CHUNK = 128
A_GROUPS = 16
QK_NOPE = 128
QK_ROPE = 64
V_DIM = 128
MLA_HEADS = 16
Q_LORA = 512
KV_LORA = 512
QK_DIM = QK_NOPE + QK_ROPE
ROPE_THETA = 10000.0
LOG2E = 1.4426950408889634
GELU_C = 0.7978845608028654
PROJ_CHUNK = 256
MEM_HEADS = 4
N_BRANCH = 3
LANE = 128
HEAD_PAD = 256
VMEM_LIMIT = 52 * 1024 * 1024

F32 = jnp.float32
BF16 = jnp.bfloat16


def _cparams(*sem):
    return pltpu.CompilerParams(dimension_semantics=sem, vmem_limit_bytes=VMEM_LIMIT)


def _dot(a, b):
    return jnp.dot(a, b, preferred_element_type=F32)


def _dot_nt(a, b):
    return lax.dot_general(a, b, (((1,), (1,)), ((), ())), preferred_element_type=F32)


def _gelu(x):
    half = 0.5 * x
    return half + half * jnp.tanh(x * (GELU_C + (GELU_C * 0.044715) * (x * x)))


def _sigmoid(x):
    return 0.5 + 0.5 * jnp.tanh(0.5 * x)


def _silu(x):
    half = 0.5 * x
    return half + half * jnp.tanh(half)


def _identity(x):
    return x


def _rms(xf, g):
    return xf * lax.rsqrt(jnp.mean(xf * xf, axis=-1, keepdims=True) + EPS) * g


def _rope_table_kernel(pos_ref, invf_ref, cos_ref, sin_ref):
    ang = pos_ref[...].astype(F32) * invf_ref[...]
    cos_ref[...] = jnp.cos(ang)
    sin_ref[...] = jnp.sin(ang)


def _rope_tables(positions, seq):
    half = QK_ROPE // 2
    inv_freq = 1.0 / (ROPE_THETA ** (jnp.arange(0, QK_ROPE, 2, dtype=F32) / QK_ROPE))
    cos_t, sin_t = pl.pallas_call(
        _rope_table_kernel,
        out_shape=(jax.ShapeDtypeStruct((half, seq), F32),) * 2,
        name="rope_tables",
    )(positions.reshape(1, seq), inv_freq.reshape(half, 1))
    cos, sin = cos_t.T, sin_t.T
    z32 = jnp.zeros((seq, half), F32)
    z64 = jnp.zeros((seq, 2 * half), F32)
    tab_c = jnp.concatenate([cos, cos, z64], axis=-1)
    tab_a = jnp.concatenate([sin, z32, z64], axis=-1)
    tab_b = jnp.concatenate([z32, sin, z64], axis=-1)
    return tab_c, tab_a, tab_b


def _rope(t, tab_c, tab_a, tab_b):
    down = pltpu.roll(t, 3 * LANE // 4, axis=1)
    up = pltpu.roll(t, LANE // 4, axis=1)
    return t * tab_c - down * tab_a + up * tab_b


def _proj_kernel(x_ref, g_ref, wt_ref, wg_ref, b_ref, o_ref, h_ref, *, tiles):
    j = pl.program_id(1)

    @pl.when(j == 0)
    def _():
        h_ref[...] = _rms(x_ref[...], g_ref[...]).astype(BF16)

    kinds = sorted({(act, row is None) for act, row in tiles}, key=lambda k: (k[0].__name__, k[1]))
    for act, is_gate in kinds:
        pred = functools.reduce(jnp.logical_or, [j == t for t, (a, row) in enumerate(tiles)
                                                 if a is act and (row is None) == is_gate])

        @pl.when(pred)
        def _(act=act, is_gate=is_gate):
            for lo in range(0, o_ref.shape[1], PROJ_CHUNK):
                cs = slice(lo, lo + PROJ_CHUNK)
                if is_gate:
                    acc = _dot(h_ref[...], wg_ref[:, cs]) + b_ref[:, cs]
                else:
                    acc = _dot_nt(h_ref[...], wt_ref[cs, :])
                o_ref[:, cs] = act(acc).astype(o_ref.dtype)


def _projection(x2, g_pre, w_in_t, w_gate, b_gate, tiles, tm, tn):
    seq, d = x2.shape
    n_tiles = len(tiles)

    def pick(j, values):
        return functools.reduce(lambda acc, tv: jnp.where(j == tv[0], tv[1], acc),
                                enumerate(values), values[0])

    rows, gates, row, gate = [], [], 0, 0
    for act, src in tiles:
        if src is None:
            gates.append(gate)
            gate += 1
        else:
            row = src
            gates.append(max(gate - 1, 0))
        rows.append(row)
    return pl.pallas_call(
        functools.partial(_proj_kernel, tiles=tiles),
        grid=(seq // tm, n_tiles),
        in_specs=[
            pl.BlockSpec((tm, d), lambda i, j: (i, 0)),
            pl.BlockSpec((1, d), lambda i, j: (0, 0)),
            pl.BlockSpec((pl.Element(tn), pl.Element(d)),
                         lambda i, j: (pick(j, [r // QK_ROPE for r in rows]) * QK_ROPE, 0)),
            pl.BlockSpec((d, tn), lambda i, j: (0, pick(j, gates))),
            pl.BlockSpec((1, tn), lambda i, j: (0, pick(j, gates))),
        ],
        out_specs=[pl.BlockSpec((tm, tn), lambda i, j: (i, j)),
                   pl.BlockSpec((tm, d), lambda i, j: (i, 0))],
        out_shape=[jax.ShapeDtypeStruct((seq, n_tiles * tn), BF16),
                   jax.ShapeDtypeStruct((seq, d), BF16)],
        compiler_params=_cparams("parallel", "arbitrary"),
        name="in_proj",
    )(x2, g_pre, w_in_t, w_gate, b_gate)


def _branch_a_kernel(gu_ref, gv_ref, sz_ref, lng_ref, lnb_ref, ws_ref, bs_ref, o_ref, *, tm):
    gv = gv_ref[...].astype(F32)
    mu = jnp.mean(gv, axis=-1, keepdims=True)
    cen = gv - mu
    var = jnp.mean(cen * cen, axis=-1, keepdims=True)
    vn = (cen * lax.rsqrt(var + EPS) * lng_ref[...] + lnb_ref[...]).astype(BF16)
    row = lax.broadcasted_iota(jnp.int32, (CHUNK, CHUNK), 0)
    col = lax.broadcasted_iota(jnp.int32, (CHUNK, CHUNK), 1)
    causal = col <= row
    for g in range(A_GROUPS):
        ws = jnp.where(causal, ws_ref[g], 0.0).astype(BF16)
        cs = slice(g * CHUNK, (g + 1) * CHUNK)
        for c in range(tm // CHUNK):
            rs = slice(c * CHUNK, (c + 1) * CHUNK)
            sv = _dot(ws, vn[rs, cs]) + bs_ref[g]
            y = gu_ref[rs, cs].astype(F32) * sv * sz_ref[rs, cs].astype(F32)
            o_ref[rs, cs] = y.astype(o_ref.dtype)


def _branch_a(proj, ln_g, ln_b, w_s, b_s_b, offs, tm):
    seq = proj.shape[0]
    w = A_GROUPS * CHUNK
    blk = lambda off: pl.BlockSpec((tm, w), lambda i, off=off: (i, off // w))
    const = lambda shape: pl.BlockSpec(shape, lambda i: (0,) * len(shape))
    return pl.pallas_call(
        functools.partial(_branch_a_kernel, tm=tm),
        grid=(seq // tm,),
        in_specs=[blk(offs["u"]), blk(offs["v"]), blk(offs["za"]),
                  const((1, w)), const((1, w)),
                  const((A_GROUPS, CHUNK, CHUNK)), const((A_GROUPS, CHUNK, CHUNK))],
        out_specs=pl.BlockSpec((tm, w), lambda i: (i, 0)),
        out_shape=jax.ShapeDtypeStruct((seq, w), BF16),
        compiler_params=_cparams("parallel"),
        name="branch_a",
    )(proj, proj, proj, ln_g, ln_b, w_s, b_s_b)


def _mla_prep_kernel(cq_ref, ckv_ref, h_ref, wkr_ref, tc_ref, ta_ref, tb_ref, qg_ref, kvg_ref,
                     wq_ref, wkv_ref, q_ref, k_ref, vt_ref):
    scale = QK_DIM ** -0.5 * LOG2E
    tab_c, tab_a, tab_b = tc_ref[...], ta_ref[...], tb_ref[...]
    cqn = _rms(cq_ref[...].astype(F32), qg_ref[...]).astype(BF16)
    ckvn = _rms(ckv_ref[...].astype(F32), kvg_ref[...]).astype(BF16)
    kpe = _rope(_dot_nt(h_ref[...], wkr_ref[...]), tab_c, tab_a, tab_b).astype(BF16)
    for h in range(MLA_HEADS):
        lo, mid, hi = h * HEAD_PAD, h * HEAD_PAD + LANE, (h + 1) * HEAD_PAD
        q = _dot(cqn, wq_ref[:, lo:hi]) * scale
        q_ref[:, lo:mid] = q[:, :LANE].astype(BF16)
        q_ref[:, mid:hi] = _rope(q[:, LANE:], tab_c, tab_a, tab_b).astype(BF16)
        kv = _dot(ckvn, wkv_ref[:, lo:hi])
        k_ref[:, lo:mid] = kv[:, :LANE].astype(BF16)
        k_ref[:, mid:hi] = kpe
        vt_ref[h * V_DIM:(h + 1) * V_DIM, :] = kv[:, LANE:].T.astype(BF16)


def _mla_prep(proj, h, w_kr, tabs, q_norm_g, kv_norm_g, wq_pad, wkv, offs, tm):
    seq, d = h.shape
    hw = MLA_HEADS * HEAD_PAD
    const = lambda shape: pl.BlockSpec(shape, lambda i: (0,) * len(shape))
    col = lambda off, w: pl.BlockSpec((tm, w), lambda i: (i, off // w))
    tab = pl.BlockSpec((tm, LANE), lambda i: (i, 0))
    return pl.pallas_call(
        _mla_prep_kernel,
        grid=(seq // tm,),
        in_specs=[col(offs["cq"], Q_LORA), col(offs["ckv"], KV_LORA),
                  pl.BlockSpec((tm, d), lambda i: (i, 0)), const((LANE, d)),
                  tab, tab, tab, const((1, Q_LORA)), const((1, KV_LORA)),
                  const((Q_LORA, hw)), const((KV_LORA, hw))],
        out_specs=[pl.BlockSpec((tm, hw), lambda i: (i, 0)),
                   pl.BlockSpec((tm, hw), lambda i: (i, 0)),
                   pl.BlockSpec((MLA_HEADS * V_DIM, tm), lambda i: (0, i))],
        out_shape=[jax.ShapeDtypeStruct((seq, hw), BF16),
                   jax.ShapeDtypeStruct((seq, hw), BF16),
                   jax.ShapeDtypeStruct((MLA_HEADS * V_DIM, seq), BF16)],
        compiler_params=_cparams("parallel"),
        name="mla_prep",
    )(proj, proj, h, w_kr, *tabs, q_norm_g, kv_norm_g, wq_pad, wkv)


def _flash_kernel(q_ref, k_ref, vt_ref, g_ref, o_ref, *scratch, tq, tk, cb):
    qi = pl.program_id(1)
    ncb = tq // cb
    take = lambda n: [scratch[i * ncb:(i + 1) * ncb] for i in range(n)]
    m_refs, l_refs, acc_refs, s0, s1, mn0, mn1, al0, al1 = take(9)
    s_refs, mnew_refs, alpha_refs = (s0, s1), (mn0, mn1), (al0, al1)
    for c in range(ncb):
        m_refs[c][...] = jnp.full(m_refs[c].shape, -1e30, F32)
        l_refs[c][...] = jnp.zeros(l_refs[c].shape, F32)
        acc_refs[c][...] = jnp.zeros(acc_refs[c].shape, F32)

    def scores(c, kstart, rows, slot, mask_off=None):
        s = _dot_nt(k_ref[pl.ds(kstart, rows), :], q_ref[c * cb:(c + 1) * cb, :])
        if mask_off is not None:
            key = lax.broadcasted_iota(jnp.int32, s.shape, 0)
            qry = lax.broadcasted_iota(jnp.int32, s.shape, 1)
            s = jnp.where(key <= qry + mask_off, s, -1e30)
        m_prev = m_refs[c][...]
        m_new = jnp.maximum(m_prev, jnp.max(s, axis=0, keepdims=True))
        s_refs[slot][c][:rows, :] = s
        mnew_refs[slot][c][...] = m_new
        alpha_refs[slot][c][...] = jnp.exp2(m_prev - m_new)
        m_refs[c][...] = m_new

    def update(c, kstart, rows, slot):
        alpha = alpha_refs[slot][c][...]
        p = jnp.exp2(s_refs[slot][c][:rows, :] - mnew_refs[slot][c][...])
        l_refs[c][...] = alpha * l_refs[c][...] + jnp.sum(p, axis=0, keepdims=True)
        pv = _dot(vt_ref[:, pl.ds(kstart, rows)], p.astype(BF16))
        acc_refs[c][...] = alpha * acc_refs[c][...] + pv

    def kv(t):
        return pl.multiple_of(t * tk, tk)

    assert tq == 2 * tk

    @pl.when(qi > 0)
    def _():
        for c in range(ncb):
            scores(c, 0, tk, 0)

        def pair(i):
            for c in range(ncb):
                scores(c, kv(2 * i + 1), tk, 1)
                update(c, kv(2 * i), tk, 0)
            for c in range(ncb):
                scores(c, kv(2 * i + 2), tk, 0)
                update(c, kv(2 * i + 1), tk, 1)

        def body(t, carry):
            pair(2 * t)
            pair(2 * t + 1)
            return carry

        lax.fori_loop(0, lax.shift_right_logical(qi - 1, 1), body, 0)

        @pl.when(((qi - 1) & 1) == 1)
        def _():
            pair(qi - 2)

        for c in range(ncb):
            scores(c, kv(2 * qi - 1), tk, 1)
            update(c, kv(2 * qi - 2), tk, 0)

    @pl.when(qi == 0)
    def _():
        for c in range(ncb):
            s_refs[1][c][...] = jnp.zeros(s_refs[1][c].shape, F32)
            mnew_refs[1][c][...] = jnp.full(mnew_refs[1][c].shape, 1e30, F32)
            alpha_refs[1][c][...] = jnp.ones(alpha_refs[1][c].shape, F32)

    pending = kv(jnp.maximum(2 * qi - 1, 0))
    base = pl.multiple_of(qi * tq, tq)
    diag = [[], []]
    for d in range(2):
        for c in range(ncb):
            rows = min(tk, (c + 1) * cb - d * tk)
            if rows > 0:
                crosses = d * tk + rows - 1 > c * cb
                diag[d].append(((c, pl.multiple_of(base + d * tk, tk), rows, d),
                                c * cb - d * tk if crosses else None))
    assert [item[0] for item, _ in diag[0]] == list(range(ncb))
    for item, mask_off in diag[0]:
        scores(*item, mask_off)
        update(item[0], pending, tk, 1)
    for n, (item, mask_off) in enumerate(diag[1]):
        scores(*item, mask_off)
        update(*diag[0][n][0])
    for item, _ in diag[0][len(diag[1]):] + diag[1]:
        update(*item)

    for c in range(ncb):
        o = (acc_refs[c][...] / l_refs[c][...]).T
        rs = slice(c * cb, (c + 1) * cb)
        o_ref[rs, :] = (o * g_ref[rs, :].astype(F32)).astype(o_ref.dtype)


def _flash(q, k, vt, proj, zb_off, tq, tk, cb):
    seq = q.shape[0]
    ncb = tq // cb
    stat = [pltpu.VMEM((1, cb), F32)] * ncb
    scratch = (stat + stat + [pltpu.VMEM((V_DIM, cb), F32)] * ncb
               + [pltpu.VMEM((tk, cb), F32)] * (2 * ncb)
               + stat * 4)
    return pl.pallas_call(
        functools.partial(_flash_kernel, tq=tq, tk=tk, cb=cb),
        grid=(MLA_HEADS, seq // tq),
        in_specs=[pl.BlockSpec((tq, HEAD_PAD), lambda h, i: (i, h)),
                  pl.BlockSpec((seq, HEAD_PAD), lambda h, i: (0, h)),
                  pl.BlockSpec((V_DIM, seq), lambda h, i: (h, 0)),
                  pl.BlockSpec((tq, V_DIM), lambda h, i: (i, zb_off // V_DIM + h))],
        out_specs=pl.BlockSpec((tq, V_DIM), lambda h, i: (i, h)),
        out_shape=jax.ShapeDtypeStruct((seq, MLA_HEADS * V_DIM), BF16),
        scratch_shapes=scratch,
        compiler_params=_cparams("parallel", "arbitrary"),
        name="flash_mla",
    )(q, k, vt, proj)


def _mem_kv_kernel(mem_ref, g_ref, w_ref, o_ref):
    mn = _rms(mem_ref[...], g_ref[...]).astype(BF16)
    o_ref[...] = _dot(mn, w_ref[...].astype(BF16)).astype(o_ref.dtype)


def _mem_kv(mem2, mem_norm_g, w_mem_kv, tn):
    m, d = mem2.shape
    n = w_mem_kv.shape[1]
    return pl.pallas_call(
        _mem_kv_kernel,
        grid=(n // tn,),
        in_specs=[pl.BlockSpec((m, d), lambda j: (0, 0)),
                  pl.BlockSpec((1, d), lambda j: (0, 0)),
                  pl.BlockSpec((d, tn), lambda j: (0, j))],
        out_specs=pl.BlockSpec((m, tn), lambda j: (0, j)),
        out_shape=jax.ShapeDtypeStruct((m, n), BF16),
        compiler_params=_cparams("parallel"),
        name="mem_kv",
    )(mem2, mem_norm_g, w_mem_kv)


def _mem_attn_kernel(q_ref, kv_ref, sz_ref, o_ref, *, width):
    hd = width // MEM_HEADS
    scale = hd ** -0.5
    for h in range(MEM_HEADS):
        cs = slice(h * hd, (h + 1) * hd)
        s = _dot_nt(q_ref[:, cs], kv_ref[:, cs]) * scale
        p = jnp.exp(s - jnp.max(s, axis=-1, keepdims=True))
        l = jnp.sum(p, axis=-1, keepdims=True)
        o = _dot(p.astype(BF16), kv_ref[:, width + h * hd:width + (h + 1) * hd]) / l
        o_ref[:, cs] = (o * sz_ref[:, cs].astype(F32)).astype(o_ref.dtype)


def _mem_attn(proj, kvm, offs, width, tm):
    seq = proj.shape[0]
    blk = lambda off: pl.BlockSpec((tm, width), lambda i: (i, off // width))
    return pl.pallas_call(
        functools.partial(_mem_attn_kernel, width=width),
        grid=(seq // tm,),
        in_specs=[blk(offs["qm"]), pl.BlockSpec(kvm.shape, lambda i: (0, 0)), blk(offs["zm"])],
        out_specs=pl.BlockSpec((tm, width), lambda i: (i, 0)),
        out_shape=jax.ShapeDtypeStruct((seq, width), BF16),
        compiler_params=_cparams("parallel"),
        name="mem_attn",
    )(proj, kvm, proj)


def _merge_kernel(ya_ref, yb_ref, ym_ref, w_ref, g0_ref, g1_ref, g2_ref, o_ref):
    acc = g0_ref[...].astype(F32) * _dot(ya_ref[...], w_ref[0])
    acc += g1_ref[...].astype(F32) * _dot(yb_ref[...], w_ref[1])
    acc += g2_ref[...].astype(F32) * _dot(ym_ref[...], w_ref[2])
    o_ref[...] = acc.astype(o_ref.dtype)


def _merge(y_a, y_b, y_m, w_branch, proj, gate_off, tm, tn):
    seq, width = y_a.shape
    d = w_branch.shape[2]
    yblk = pl.BlockSpec((tm, width), lambda i, j: (i, 0))
    gblk = lambda n: pl.BlockSpec((tm, tn), lambda i, j, n=n: (i, (gate_off + n * d) // tn + j))
    return pl.pallas_call(
        _merge_kernel,
        grid=(seq // tm, d // tn),
        in_specs=[yblk, yblk, yblk,
                  pl.BlockSpec((N_BRANCH, width, tn), lambda i, j: (0, 0, j)),
                  gblk(0), gblk(1), gblk(2)],
        out_specs=pl.BlockSpec((tm, tn), lambda i, j: (i, j)),
        out_shape=jax.ShapeDtypeStruct((seq, d), BF16),
        compiler_params=_cparams("parallel", "arbitrary"),
        name="merge",
    )(y_a, y_b, y_m, w_branch, proj, proj, proj)


def _out_kernel(m_ref, w_ref, x_ref, g_ref, o_ref):
    out = _dot(m_ref[...], w_ref[...])
    o_ref[...] = x_ref[...] + _rms(out, g_ref[...])


def _out_proj(merged, w_out, x2, g_post, tm):
    seq, d = x2.shape
    return pl.pallas_call(
        _out_kernel,
        grid=(seq // tm,),
        in_specs=[pl.BlockSpec((tm, d), lambda i: (i, 0)),
                  pl.BlockSpec((d, d), lambda i: (0, 0)),
                  pl.BlockSpec((tm, d), lambda i: (i, 0)),
                  pl.BlockSpec((1, d), lambda i: (0, 0))],
        out_specs=pl.BlockSpec((tm, d), lambda i: (i, 0)),
        out_shape=jax.ShapeDtypeStruct((seq, d), F32),
        compiler_params=_cparams("parallel"),
        name="out_proj",
    )(merged, w_out, x2, g_post)


def _layer(x2, mem2, tabs, g_pre, w_in, a_ln_g, a_ln_b, a_w_s, a_b_s, q_norm_g, w_uq,
           kv_norm_g, w_ukv, mem_norm_g, w_mem_kv, w_gate, b_gate, w_branch, w_out, g_post):
    seq, d = x2.shape
    tn = 1024
    w_in_t = jnp.swapaxes(w_in, 0, 1).astype(BF16)
    lat, tail = 3 * d, 3 * d + Q_LORA + KV_LORA
    w_kr = jnp.pad(w_in_t[tail:tail + QK_ROPE], ((0, LANE - QK_ROPE), (0, 0)))
    sections = (("u", _gelu, 0, d), ("v", _gelu, d, d), ("za", _silu, 2 * d, d),
                ("zb", _silu, tail + QK_ROPE, d), ("qm", _identity, tail + QK_ROPE + d, d),
                ("zm", _silu, tail + QK_ROPE + 2 * d, d), ("gate", _sigmoid, None, N_BRANCH * d),
                ("cq", _identity, lat, Q_LORA + KV_LORA))
    offs, tiles, col = {}, [], 0
    for name, act, row, width in sections:
        offs[name] = col
        tiles += [(act, None if row is None else row + t) for t in range(0, width, tn)]
        col += width
    offs["ckv"] = offs["cq"] + Q_LORA

    proj, h = _projection(x2, g_pre.reshape(1, d), w_in_t, w_gate.astype(BF16),
                          b_gate.reshape(1, -1), tuple(tiles), tm=1024, tn=tn)

    b_s_b = jnp.broadcast_to(a_b_s[:, :, None], (A_GROUPS, CHUNK, CHUNK))
    y_a = _branch_a(proj, a_ln_g.reshape(1, d), a_ln_b.reshape(1, d), a_w_s, b_s_b, offs, tm=512)

    wq_pad = jnp.pad(w_uq.reshape(Q_LORA, MLA_HEADS, QK_DIM),
                     ((0, 0), (0, 0), (0, HEAD_PAD - QK_DIM))).reshape(Q_LORA, -1).astype(BF16)
    q, k, vv = _mla_prep(proj, h, w_kr, tabs, q_norm_g.reshape(1, -1), kv_norm_g.reshape(1, -1),
                         wq_pad, w_ukv.astype(BF16), offs, tm=512)
    y_b = _flash(q, k, vv, proj, offs["zb"], tq=1024, tk=512, cb=256)

    kvm = _mem_kv(mem2, mem_norm_g.reshape(1, d), w_mem_kv, tn=512)
    y_m = _mem_attn(proj, kvm, offs, d, tm=512)

    merged = _merge(y_a, y_b, y_m, w_branch.astype(BF16), proj, offs["gate"], tm=1024, tn=512)
    return _out_proj(merged, w_out.astype(BF16), x2, g_post.reshape(1, d), tm=512)


def kernel(x, mem, positions, g_pre, w_in, a_ln_g, a_ln_b, a_w_s, a_b_s, q_norm_g, w_uq,
           kv_norm_g, w_ukv, mem_norm_g, w_mem_kv, w_gate, b_gate, w_branch, w_out, g_post):
    batch, seq, d = x.shape
    depth = g_pre.shape[0]
    outs = []
    for b in range(batch):
        tabs = _rope_tables(positions[b], seq)
        xb = x[b]
        for l in range(depth):
            xb = _layer(xb, mem[b], tabs, g_pre[l], w_in[l], a_ln_g[l], a_ln_b[l], a_w_s[l],
                        a_b_s[l], q_norm_g[l], w_uq[l], kv_norm_g[l], w_ukv[l], mem_norm_g[l],
                        w_mem_kv[l], w_gate[l], b_gate[l], w_branch[l], w_out[l], g_post[l])
        outs.append(xb)
    return jnp.stack(outs, axis=0)
```

```python
import functools

import jax
import jax.numpy as jnp
from jax import lax
from jax.experimental import pallas as pl
from jax.experimental.pallas import tpu as pltpu

EPS = 1e-6
CHUNK = 128
A_GROUPS = 16
QK_NOPE = 128
QK_ROPE = 64
V_DIM = 128
MLA_HEADS = 16
Q_LORA = 512
KV_LORA = 512
QK_DIM = QK_NOPE + QK_ROPE
ROPE_THETA = 10000.0
LOG2E = 1.4426950408889634
GELU_C = 0.7978845608028654
PROJ_CHUNK = 256
MEM_HEADS = 4
N_BRANCH = 3
LANE = 128
HEAD_PAD = 256
VMEM_LIMIT = 52 * 1024 * 1024

F32 = jnp.float32
BF16 = jnp.bfloat16


def _cparams(*sem):
    return pltpu.CompilerParams(dimension_semantics=sem, vmem_limit_bytes=VMEM_LIMIT)


def _dot(a, b):
    return jnp.dot(a, b, preferred_element_type=F32)


def _dot_nt(a, b):
    return lax.dot_general(a, b, (((1,), (1,)), ((), ())), preferred_element_type=F32)


def _gelu(x):
    half = 0.5 * x
    return half + half * jnp.tanh(x * (GELU_C + (GELU_C * 0.044715) * (x * x)))


def _sigmoid(x):
    return 0.5 + 0.5 * jnp.tanh(0.5 * x)


def _silu(x):
    half = 0.5 * x
    return half + half * jnp.tanh(half)


def _identity(x):
    return x


def _rms(xf, g):
    return xf * lax.rsqrt(jnp.mean(xf * xf, axis=-1, keepdims=True) + EPS) * g


def _rope_table_kernel(pos_ref, invf_ref, tc_ref, ta_ref, tb_ref):
    ang = pos_ref[...].astype(F32) * invf_ref[...]
    cos, sin = jnp.cos(ang), jnp.sin(ang)
    z32 = jnp.zeros_like(cos)
    z64 = jnp.concatenate([z32, z32], axis=0)
    tc_ref[...] = jnp.concatenate([cos, cos, z64], axis=0).T
    ta_ref[...] = jnp.concatenate([sin, z32, z64], axis=0).T
    tb_ref[...] = jnp.concatenate([z32, sin, z64], axis=0).T


def _rope_tables(positions, seq):
    half = QK_ROPE // 2
    inv_freq = 1.0 / (ROPE_THETA ** (jnp.arange(0, QK_ROPE, 2, dtype=F32) / QK_ROPE))
    return pl.pallas_call(
        _rope_table_kernel,
        out_shape=(jax.ShapeDtypeStruct((seq, LANE), F32),) * 3,
        compiler_params=pltpu.CompilerParams(vmem_limit_bytes=VMEM_LIMIT),
        name="rope_tables",
    )(positions.reshape(1, seq), inv_freq.reshape(half, 1))


def _rope(t, tab_c, tab_a, tab_b):
    down = pltpu.roll(t, 3 * LANE // 4, axis=1)
    up = pltpu.roll(t, LANE // 4, axis=1)
    return t * tab_c - down * tab_a + up * tab_b


def _proj_kernel(x_ref, g_ref, wt_ref, wg_ref, b_ref, o_ref, h_ref, *, tiles):
    j = pl.program_id(1)

    @pl.when(j == 0)
    def _():
        h_ref[...] = _rms(x_ref[...], g_ref[...]).astype(BF16)

    kinds = sorted({(act, row is None) for act, row in tiles}, key=lambda k: (k[0].__name__, k[1]))
    for act, is_gate in kinds:
        pred = functools.reduce(jnp.logical_or, [j == t for t, (a, row) in enumerate(tiles)
                                                 if a is act and (row is None) == is_gate])

        @pl.when(pred)
        def _(act=act, is_gate=is_gate):
            for lo in range(0, o_ref.shape[1], PROJ_CHUNK):
                cs = slice(lo, lo + PROJ_CHUNK)
                if is_gate:
                    acc = _dot(h_ref[...], wg_ref[:, cs]) + b_ref[:, cs]
                else:
                    acc = _dot_nt(h_ref[...], wt_ref[cs, :])
                o_ref[:, cs] = act(acc).astype(o_ref.dtype)


def _projection(x2, g_pre, w_in_t, w_gate, b_gate, tiles, tm, tn):
    seq, d = x2.shape
    n_tiles = len(tiles)

    def pick(j, values):
        return functools.reduce(lambda acc, tv: jnp.where(j == tv[0], tv[1], acc),
                                enumerate(values), values[0])

    rows, gates, row, gate = [], [], 0, 0
    for act, src in tiles:
        if src is None:
            gates.append(gate)
            gate += 1
        else:
            row = src
            gates.append(max(gate - 1, 0))
        rows.append(row)
    return pl.pallas_call(
        functools.partial(_proj_kernel, tiles=tiles),
        grid=(seq // tm, n_tiles),
        in_specs=[
            pl.BlockSpec((tm, d), lambda i, j: (i, 0)),
            pl.BlockSpec((1, d), lambda i, j: (0, 0)),
            pl.BlockSpec((pl.Element(tn), pl.Element(d)),
                         lambda i, j: (pick(j, [r // QK_ROPE for r in rows]) * QK_ROPE, 0)),
            pl.BlockSpec((d, tn), lambda i, j: (0, pick(j, gates))),
            pl.BlockSpec((1, tn), lambda i, j: (0, pick(j, gates))),
        ],
        out_specs=[pl.BlockSpec((tm, tn), lambda i, j: (i, j)),
                   pl.BlockSpec((tm, d), lambda i, j: (i, 0))],
        out_shape=[jax.ShapeDtypeStruct((seq, n_tiles * tn), BF16),
                   jax.ShapeDtypeStruct((seq, d), BF16)],
        compiler_params=_cparams("parallel", "arbitrary"),
        name="in_proj",
    )(x2, g_pre, w_in_t, w_gate, b_gate)


def _branch_a_kernel(gu_ref, gv_ref, sz_ref, lng_ref, lnb_ref, ws_ref, bs_ref, o_ref, *, tm):
    gv = gv_ref[...].astype(F32)
    mu = jnp.mean(gv, axis=-1, keepdims=True)
    cen = gv - mu
    var = jnp.mean(cen * cen, axis=-1, keepdims=True)
    vn = (cen * lax.rsqrt(var + EPS) * lng_ref[...] + lnb_ref[...]).astype(BF16)
    row = lax.broadcasted_iota(jnp.int32, (CHUNK, CHUNK), 0)
    col = lax.broadcasted_iota(jnp.int32, (CHUNK, CHUNK), 1)
    causal = col <= row
    for g in range(A_GROUPS):
        ws = jnp.where(causal, ws_ref[g], 0.0).astype(BF16)
        cs = slice(g * CHUNK, (g + 1) * CHUNK)
        for c in range(tm // CHUNK):
            rs = slice(c * CHUNK, (c + 1) * CHUNK)
            sv = _dot(ws, vn[rs, cs]) + bs_ref[g]
            y = gu_ref[rs, cs].astype(F32) * sv * sz_ref[rs, cs].astype(F32)
            o_ref[rs, cs] = y.astype(o_ref.dtype)


def _branch_a(proj, ln_g, ln_b, w_s, b_s_b, offs, tm):
    seq = proj.shape[0]
    w = A_GROUPS * CHUNK
    blk = lambda off: pl.BlockSpec((tm, w), lambda i, off=off: (i, off // w))
    const = lambda shape: pl.BlockSpec(shape, lambda i: (0,) * len(shape))
    return pl.pallas_call(
        functools.partial(_branch_a_kernel, tm=tm),
        grid=(seq // tm,),
        in_specs=[blk(offs["u"]), blk(offs["v"]), blk(offs["za"]),
                  const((1, w)), const((1, w)),
                  const((A_GROUPS, CHUNK, CHUNK)), const((A_GROUPS, CHUNK, CHUNK))],
        out_specs=pl.BlockSpec((tm, w), lambda i: (i, 0)),
        out_shape=jax.ShapeDtypeStruct((seq, w), BF16),
        compiler_params=_cparams("parallel"),
        name="branch_a",
    )(proj, proj, proj, ln_g, ln_b, w_s, b_s_b)


def _mla_prep_kernel(cq_ref, ckv_ref, h_ref, wkr_ref, tc_ref, ta_ref, tb_ref, qg_ref, kvg_ref,
                     wq_ref, wkv_ref, q_ref, k_ref, vt_ref):
    scale = QK_DIM ** -0.5 * LOG2E
    tab_c, tab_a, tab_b = tc_ref[...], ta_ref[...], tb_ref[...]
    cqn = _rms(cq_ref[...].astype(F32), qg_ref[...]).astype(BF16)
    ckvn = _rms(ckv_ref[...].astype(F32), kvg_ref[...]).astype(BF16)
    kpe = _rope(_dot_nt(h_ref[...], wkr_ref[...]), tab_c, tab_a, tab_b).astype(BF16)
    for h in range(MLA_HEADS):
        lo, mid, hi = h * HEAD_PAD, h * HEAD_PAD + LANE, (h + 1) * HEAD_PAD
        q = _dot(cqn, wq_ref[:, lo:hi]) * scale
        q_ref[:, lo:mid] = q[:, :LANE].astype(BF16)
        q_ref[:, mid:hi] = _rope(q[:, LANE:], tab_c, tab_a, tab_b).astype(BF16)
        kv = _dot(ckvn, wkv_ref[:, lo:hi])
        k_ref[:, lo:mid] = kv[:, :LANE].astype(BF16)
        k_ref[:, mid:hi] = kpe
        vt_ref[h * V_DIM:(h + 1) * V_DIM, :] = kv[:, LANE:].T.astype(BF16)


def _mla_prep(proj, h, w_kr, tabs, q_norm_g, kv_norm_g, wq_pad, wkv, offs, tm):
    seq, d = h.shape
    hw = MLA_HEADS * HEAD_PAD
    const = lambda shape: pl.BlockSpec(shape, lambda i: (0,) * len(shape))
    col = lambda off, w: pl.BlockSpec((tm, w), lambda i: (i, off // w))
    tab = pl.BlockSpec((tm, LANE), lambda i: (i, 0))
    return pl.pallas_call(
        _mla_prep_kernel,
        grid=(seq // tm,),
        in_specs=[col(offs["cq"], Q_LORA), col(offs["ckv"], KV_LORA),
                  pl.BlockSpec((tm, d), lambda i: (i, 0)), const((LANE, d)),
                  tab, tab, tab, const((1, Q_LORA)), const((1, KV_LORA)),
                  const((Q_LORA, hw)), const((KV_LORA, hw))],
        out_specs=[pl.BlockSpec((tm, hw), lambda i: (i, 0)),
                   pl.BlockSpec((tm, hw), lambda i: (i, 0)),
                   pl.BlockSpec((MLA_HEADS * V_DIM, tm), lambda i: (0, i))],
        out_shape=[jax.ShapeDtypeStruct((seq, hw), BF16),
                   jax.ShapeDtypeStruct((seq, hw), BF16),
                   jax.ShapeDtypeStruct((MLA_HEADS * V_DIM, seq), BF16)],
        compiler_params=_cparams("parallel"),
        name="mla_prep",
    )(proj, proj, h, w_kr, *tabs, q_norm_g, kv_norm_g, wq_pad, wkv)


def _flash_kernel(q_ref, k_ref, vt_ref, g_ref, o_ref, *scratch, tq, tk, cb):
    qi = pl.program_id(1)
    ncb = tq // cb
    take = lambda n: [scratch[i * ncb:(i + 1) * ncb] for i in range(n)]
    m_refs, l_refs, acc_refs, s0, s1, mn0, mn1, al0, al1 = take(9)
    s_refs, mnew_refs, alpha_refs = (s0, s1), (mn0, mn1), (al0, al1)
    for c in range(ncb):
        m_refs[c][...] = jnp.full(m_refs[c].shape, -1e30, F32)
        l_refs[c][...] = jnp.zeros(l_refs[c].shape, F32)
        acc_refs[c][...] = jnp.zeros(acc_refs[c].shape, F32)

    def scores(c, kstart, rows, slot, mask_off=None):
        s = _dot_nt(k_ref[pl.ds(kstart, rows), :], q_ref[c * cb:(c + 1) * cb, :])
        if mask_off is not None:
            key = lax.broadcasted_iota(jnp.int32, s.shape, 0)
            qry = lax.broadcasted_iota(jnp.int32, s.shape, 1)
            s = jnp.where(key <= qry + mask_off, s, -1e30)
        m_prev = m_refs[c][...]
        m_new = jnp.maximum(m_prev, jnp.max(s, axis=0, keepdims=True))
        s_refs[slot][c][:rows, :] = s
        mnew_refs[slot][c][...] = m_new
        alpha_refs[slot][c][...] = jnp.exp2(m_prev - m_new)
        m_refs[c][...] = m_new

    def update(c, kstart, rows, slot):
        alpha = alpha_refs[slot][c][...]
        p = jnp.exp2(s_refs[slot][c][:rows, :] - mnew_refs[slot][c][...])
        l_refs[c][...] = alpha * l_refs[c][...] + jnp.sum(p, axis=0, keepdims=True)
        pv = _dot(vt_ref[:, pl.ds(kstart, rows)], p.astype(BF16))
        acc_refs[c][...] = alpha * acc_refs[c][...] + pv

    def kv(t):
        return pl.multiple_of(t * tk, tk)

    assert tq == 2 * tk

    @pl.when(qi > 0)
    def _():
        for c in range(ncb):
            scores(c, 0, tk, 0)

        def pair(i):
            for c in range(ncb):
                scores(c, kv(2 * i + 1), tk, 1)
                update(c, kv(2 * i), tk, 0)
            for c in range(ncb):
                scores(c, kv(2 * i + 2), tk, 0)
                update(c, kv(2 * i + 1), tk, 1)

        def body(t, carry):
            pair(2 * t)
            pair(2 * t + 1)
            return carry

        lax.fori_loop(0, lax.shift_right_logical(qi - 1, 1), body, 0)

        @pl.when(((qi - 1) & 1) == 1)
        def _():
            pair(qi - 2)

        for c in range(ncb):
            scores(c, kv(2 * qi - 1), tk, 1)
            update(c, kv(2 * qi - 2), tk, 0)

    @pl.when(qi == 0)
    def _():
        for c in range(ncb):
            s_refs[1][c][...] = jnp.zeros(s_refs[1][c].shape, F32)
            mnew_refs[1][c][...] = jnp.full(mnew_refs[1][c].shape, 1e30, F32)
            alpha_refs[1][c][...] = jnp.ones(alpha_refs[1][c].shape, F32)

    pending = kv(jnp.maximum(2 * qi - 1, 0))
    base = pl.multiple_of(qi * tq, tq)
    diag = [[], []]
    for d in range(2):
        for c in range(ncb):
            rows = min(tk, (c + 1) * cb - d * tk)
            if rows > 0:
                crosses = d * tk + rows - 1 > c * cb
                diag[d].append(((c, pl.multiple_of(base + d * tk, tk), rows, d),
                                c * cb - d * tk if crosses else None))
    assert [item[0] for item, _ in diag[0]] == list(range(ncb))
    for item, mask_off in diag[0]:
        scores(*item, mask_off)
        update(item[0], pending, tk, 1)
    for n, (item, mask_off) in enumerate(diag[1]):
        scores(*item, mask_off)
        update(*diag[0][n][0])
    for item, _ in diag[0][len(diag[1]):] + diag[1]:
        update(*item)

    for c in range(ncb):
        o = (acc_refs[c][...] / l_refs[c][...]).T
        rs = slice(c * cb, (c + 1) * cb)
        o_ref[rs, :] = (o * g_ref[rs, :].astype(F32)).astype(o_ref.dtype)


def _flash(q, k, vt, proj, zb_off, tq, tk, cb):
    seq = q.shape[0]
    ncb = tq // cb
    stat = [pltpu.VMEM((1, cb), F32)] * ncb
    scratch = (stat + stat + [pltpu.VMEM((V_DIM, cb), F32)] * ncb
               + [pltpu.VMEM((tk, cb), F32)] * (2 * ncb)
               + stat * 4)
    return pl.pallas_call(
        functools.partial(_flash_kernel, tq=tq, tk=tk, cb=cb),
        grid=(MLA_HEADS, seq // tq),
        in_specs=[pl.BlockSpec((tq, HEAD_PAD), lambda h, i: (i, h)),
                  pl.BlockSpec((seq, HEAD_PAD), lambda h, i: (0, h)),
                  pl.BlockSpec((V_DIM, seq), lambda h, i: (h, 0)),
                  pl.BlockSpec((tq, V_DIM), lambda h, i: (i, zb_off // V_DIM + h))],
        out_specs=pl.BlockSpec((tq, V_DIM), lambda h, i: (i, h)),
        out_shape=jax.ShapeDtypeStruct((seq, MLA_HEADS * V_DIM), BF16),
        scratch_shapes=scratch,
        compiler_params=_cparams("parallel", "arbitrary"),
        name="flash_mla",
    )(q, k, vt, proj)


def _mem_kv_kernel(mem_ref, g_ref, w_ref, o_ref):
    mn = _rms(mem_ref[...], g_ref[...]).astype(BF16)
    o_ref[...] = _dot(mn, w_ref[...].astype(BF16)).astype(o_ref.dtype)


def _mem_kv(mem2, mem_norm_g, w_mem_kv, tn):
    m, d = mem2.shape
    n = w_mem_kv.shape[1]
    return pl.pallas_call(
        _mem_kv_kernel,
        grid=(n // tn,),
        in_specs=[pl.BlockSpec((m, d), lambda j: (0, 0)),
                  pl.BlockSpec((1, d), lambda j: (0, 0)),
                  pl.BlockSpec((d, tn), lambda j: (0, j))],
        out_specs=pl.BlockSpec((m, tn), lambda j: (0, j)),
        out_shape=jax.ShapeDtypeStruct((m, n), BF16),
        compiler_params=_cparams("parallel"),
        name="mem_kv",
    )(mem2, mem_norm_g, w_mem_kv)


def _mem_attn_kernel(q_ref, kv_ref, sz_ref, o_ref, *, width):
    hd = width // MEM_HEADS
    scale = hd ** -0.5
    for h in range(MEM_HEADS):
        cs = slice(h * hd, (h + 1) * hd)
        s = _dot_nt(q_ref[:, cs], kv_ref[:, cs]) * scale
        p = jnp.exp(s - jnp.max(s, axis=-1, keepdims=True))
        l = jnp.sum(p, axis=-1, keepdims=True)
        o = _dot(p.astype(BF16), kv_ref[:, width + h * hd:width + (h + 1) * hd]) / l
        o_ref[:, cs] = (o * sz_ref[:, cs].astype(F32)).astype(o_ref.dtype)


def _mem_attn(proj, kvm, offs, width, tm):
    seq = proj.shape[0]
    blk = lambda off: pl.BlockSpec((tm, width), lambda i: (i, off // width))
    return pl.pallas_call(
        functools.partial(_mem_attn_kernel, width=width),
        grid=(seq // tm,),
        in_specs=[blk(offs["qm"]), pl.BlockSpec(kvm.shape, lambda i: (0, 0)), blk(offs["zm"])],
        out_specs=pl.BlockSpec((tm, width), lambda i: (i, 0)),
        out_shape=jax.ShapeDtypeStruct((seq, width), BF16),
        compiler_params=_cparams("parallel"),
        name="mem_attn",
    )(proj, kvm, proj)


def _merge_kernel(ya_ref, yb_ref, ym_ref, w_ref, g0_ref, g1_ref, g2_ref, o_ref):
    acc = g0_ref[...].astype(F32) * _dot(ya_ref[...], w_ref[0])
    acc += g1_ref[...].astype(F32) * _dot(yb_ref[...], w_ref[1])
    acc += g2_ref[...].astype(F32) * _dot(ym_ref[...], w_ref[2])
    o_ref[...] = acc.astype(o_ref.dtype)


def _merge(y_a, y_b, y_m, w_branch, proj, gate_off, tm, tn):
    seq, width = y_a.shape
    d = w_branch.shape[2]
    yblk = pl.BlockSpec((tm, width), lambda i, j: (i, 0))
    gblk = lambda n: pl.BlockSpec((tm, tn), lambda i, j, n=n: (i, (gate_off + n * d) // tn + j))
    return pl.pallas_call(
        _merge_kernel,
        grid=(seq // tm, d // tn),
        in_specs=[yblk, yblk, yblk,
                  pl.BlockSpec((N_BRANCH, width, tn), lambda i, j: (0, 0, j)),
                  gblk(0), gblk(1), gblk(2)],
        out_specs=pl.BlockSpec((tm, tn), lambda i, j: (i, j)),
        out_shape=jax.ShapeDtypeStruct((seq, d), BF16),
        compiler_params=_cparams("parallel", "arbitrary"),
        name="merge",
    )(y_a, y_b, y_m, w_branch, proj, proj, proj)


def _out_kernel(m_ref, w_ref, x_ref, g_ref, o_ref):
    out = _dot(m_ref[...], w_ref[...])
    o_ref[...] = x_ref[...] + _rms(out, g_ref[...])


def _out_proj(merged, w_out, x2, g_post, tm):
    seq, d = x2.shape
    return pl.pallas_call(
        _out_kernel,
        grid=(seq // tm,),
        in_specs=[pl.BlockSpec((tm, d), lambda i: (i, 0)),
                  pl.BlockSpec((d, d), lambda i: (0, 0)),
                  pl.BlockSpec((tm, d), lambda i: (i, 0)),
                  pl.BlockSpec((1, d), lambda i: (0, 0))],
        out_specs=pl.BlockSpec((tm, d), lambda i: (i, 0)),
        out_shape=jax.ShapeDtypeStruct((seq, d), F32),
        compiler_params=_cparams("parallel"),
        name="out_proj",
    )(merged, w_out, x2, g_post)


def _layer(x2, mem2, tabs, g_pre, w_in, a_ln_g, a_ln_b, a_w_s, a_b_s, q_norm_g, w_uq,
           kv_norm_g, w_ukv, mem_norm_g, w_mem_kv, w_gate, b_gate, w_branch, w_out, g_post):
    seq, d = x2.shape
    tn = 1024
    w_in_t = jnp.swapaxes(w_in, 0, 1).astype(BF16)
    lat, tail = 3 * d, 3 * d + Q_LORA + KV_LORA
    w_kr = jnp.pad(w_in_t[tail:tail + QK_ROPE], ((0, LANE - QK_ROPE), (0, 0)))
    sections = (("u", _gelu, 0, d), ("v", _gelu, d, d), ("za", _silu, 2 * d, d),
                ("zb", _silu, tail + QK_ROPE, d), ("qm", _identity, tail + QK_ROPE + d, d),
                ("zm", _silu, tail + QK_ROPE + 2 * d, d), ("gate", _sigmoid, None, N_BRANCH * d),
                ("cq", _identity, lat, Q_LORA + KV_LORA))
    offs, tiles, col = {}, [], 0
    for name, act, row, width in sections:
        offs[name] = col
        tiles += [(act, None if row is None else row + t) for t in range(0, width, tn)]
        col += width
    offs["ckv"] = offs["cq"] + Q_LORA

    proj, h = _projection(x2, g_pre.reshape(1, d), w_in_t, w_gate.astype(BF16),
                          b_gate.reshape(1, -1), tuple(tiles), tm=1024, tn=tn)

    b_s_b = jnp.broadcast_to(a_b_s[:, :, None], (A_GROUPS, CHUNK, CHUNK))
    y_a = _branch_a(proj, a_ln_g.reshape(1, d), a_ln_b.reshape(1, d), a_w_s, b_s_b, offs, tm=512)

    wq_pad = jnp.pad(w_uq.reshape(Q_LORA, MLA_HEADS, QK_DIM),
                     ((0, 0), (0, 0), (0, HEAD_PAD - QK_DIM))).reshape(Q_LORA, -1).astype(BF16)
    q, k, vv = _mla_prep(proj, h, w_kr, tabs, q_norm_g.reshape(1, -1), kv_norm_g.reshape(1, -1),
                         wq_pad, w_ukv.astype(BF16), offs, tm=512)
    y_b = _flash(q, k, vv, proj, offs["zb"], tq=1024, tk=512, cb=256)

    kvm = _mem_kv(mem2, mem_norm_g.reshape(1, d), w_mem_kv, tn=512)
    y_m = _mem_attn(proj, kvm, offs, d, tm=1024)

    merged = _merge(y_a, y_b, y_m, w_branch.astype(BF16), proj, offs["gate"], tm=1024, tn=512)
    return _out_proj(merged, w_out.astype(BF16), x2, g_post.reshape(1, d), tm=512)


def kernel(x, mem, positions, g_pre, w_in, a_ln_g, a_ln_b, a_w_s, a_b_s, q_norm_g, w_uq,
           kv_norm_g, w_ukv, mem_norm_g, w_mem_kv, w_gate, b_gate, w_branch, w_out, g_post):
    batch, seq, d = x.shape
    depth = g_pre.shape[0]
    outs = []
    for b in range(batch):
        tabs = _rope_tables(positions[b], seq)
        xb = x[b]
        for l in range(depth):
            xb = _layer(xb, mem[b], tabs, g_pre[l], w_in[l], a_ln_g[l], a_ln_b[l], a_w_s[l],
                        a_b_s[l], q_norm_g[l], w_uq[l], kv_norm_g[l], w_ukv[l], mem_norm_g[l],
                        w_mem_kv[l], w_gate[l], b_gate[l], w_branch[l], w_out[l], g_post[l])
        outs.append(xb)
    return jnp.stack(outs, axis=0)
```

```python
import functools

import jax
import jax.numpy as jnp
from jax import lax
from jax.experimental import pallas as pl
from jax.experimental.pallas import tpu as pltpu

EPS = 1e-6
CHUNK = 128
A_GROUPS = 16
QK_NOPE = 128
QK_ROPE = 64
V_DIM = 128
MLA_HEADS = 16
Q_LORA = 512
KV_LORA = 512
QK_DIM = QK_NOPE + QK_ROPE
ROPE_THETA = 10000.0
LOG2E = 1.4426950408889634
GELU_C = 0.7978845608028654
PROJ_CHUNK = 256
MEM_HEADS = 4
N_BRANCH = 3
LANE = 128
HEAD_PAD = 256
VMEM_LIMIT = 52 * 1024 * 1024

F32 = jnp.float32
BF16 = jnp.bfloat16


def _cparams(*sem):
    return pltpu.CompilerParams(dimension_semantics=sem, vmem_limit_bytes=VMEM_LIMIT)


def _dot(a, b):
    return jnp.dot(a, b, preferred_element_type=F32)


def _dot_nt(a, b):
    return lax.dot_general(a, b, (((1,), (1,)), ((), ())), preferred_element_type=F32)


def _gelu(x):
    half = 0.5 * x
    return half + half * jnp.tanh(x * (GELU_C + (GELU_C * 0.044715) * (x * x)))


def _sigmoid(x):
    return 0.5 + 0.5 * jnp.tanh(0.5 * x)


def _silu(x):
    half = 0.5 * x
    return half + half * jnp.tanh(half)


def _identity(x):
    return x


def _rms(xf, g):
    return xf * lax.rsqrt(jnp.mean(xf * xf, axis=-1, keepdims=True) + EPS) * g


def _rope_table_kernel(pos_ref, invf_ref, tc_ref, ta_ref, tb_ref):
    ang = pos_ref[...].astype(F32) * invf_ref[...]
    cos, sin = jnp.cos(ang), jnp.sin(ang)
    z32 = jnp.zeros_like(cos)
    z64 = jnp.concatenate([z32, z32], axis=0)
    tc_ref[...] = jnp.concatenate([cos, cos, z64], axis=0).T
    ta_ref[...] = jnp.concatenate([sin, z32, z64], axis=0).T
    tb_ref[...] = jnp.concatenate([z32, sin, z64], axis=0).T


def _rope_tables(positions, seq):
    half = QK_ROPE // 2
    inv_freq = 1.0 / (ROPE_THETA ** (jnp.arange(0, QK_ROPE, 2, dtype=F32) / QK_ROPE))
    return pl.pallas_call(
        _rope_table_kernel,
        out_shape=(jax.ShapeDtypeStruct((seq, LANE), F32),) * 3,
        compiler_params=pltpu.CompilerParams(vmem_limit_bytes=VMEM_LIMIT),
        name="rope_tables",
    )(positions.reshape(1, seq), inv_freq.reshape(half, 1))


def _rope(t, tab_c, tab_a, tab_b):
    down = pltpu.roll(t, 3 * LANE // 4, axis=1)
    up = pltpu.roll(t, LANE // 4, axis=1)
    return t * tab_c - down * tab_a + up * tab_b


def _proj_kernel(x_ref, g_ref, wt_ref, wg_ref, b_ref, o_ref, h_ref, *, tiles):
    j = pl.program_id(1)

    @pl.when(j == 0)
    def _():
        h_ref[...] = _rms(x_ref[...], g_ref[...]).astype(BF16)

    kinds = sorted({(act, row is None) for act, row in tiles}, key=lambda k: (k[0].__name__, k[1]))
    for act, is_gate in kinds:
        pred = functools.reduce(jnp.logical_or, [j == t for t, (a, row) in enumerate(tiles)
                                                 if a is act and (row is None) == is_gate])

        @pl.when(pred)
        def _(act=act, is_gate=is_gate):
            for lo in range(0, o_ref.shape[1], PROJ_CHUNK):
                cs = slice(lo, lo + PROJ_CHUNK)
                if is_gate:
                    acc = _dot(h_ref[...], wg_ref[:, cs]) + b_ref[:, cs]
                else:
                    acc = _dot_nt(h_ref[...], wt_ref[cs, :])
                o_ref[:, cs] = act(acc).astype(o_ref.dtype)


def _projection(x2, g_pre, w_in_t, w_gate, b_gate, tiles, tm, tn):
    seq, d = x2.shape
    n_tiles = len(tiles)

    def pick(j, values):
        return functools.reduce(lambda acc, tv: jnp.where(j == tv[0], tv[1], acc),
                                enumerate(values), values[0])

    rows, gates, row, gate = [], [], 0, 0
    for act, src in tiles:
        if src is None:
            gates.append(gate)
            gate += 1
        else:
            row = src
            gates.append(max(gate - 1, 0))
        rows.append(row)
    return pl.pallas_call(
        functools.partial(_proj_kernel, tiles=tiles),
        grid=(seq // tm, n_tiles),
        in_specs=[
            pl.BlockSpec((tm, d), lambda i, j: (i, 0)),
            pl.BlockSpec((1, d), lambda i, j: (0, 0)),
            pl.BlockSpec((pl.Element(tn), pl.Element(d)),
                         lambda i, j: (pick(j, [r // QK_ROPE for r in rows]) * QK_ROPE, 0)),
            pl.BlockSpec((d, tn), lambda i, j: (0, pick(j, gates))),
            pl.BlockSpec((1, tn), lambda i, j: (0, pick(j, gates))),
        ],
        out_specs=[pl.BlockSpec((tm, tn), lambda i, j: (i, j)),
                   pl.BlockSpec((tm, d), lambda i, j: (i, 0))],
        out_shape=[jax.ShapeDtypeStruct((seq, n_tiles * tn), BF16),
                   jax.ShapeDtypeStruct((seq, d), BF16)],
        compiler_params=_cparams("parallel", "arbitrary"),
        name="in_proj",
    )(x2, g_pre, w_in_t, w_gate, b_gate)


def _branch_a_kernel(gu_ref, gv_ref, sz_ref, lng_ref, lnb_ref, ws_ref, bs_ref, o_ref, *, tm):
    gv = gv_ref[...].astype(F32)
    mu = jnp.mean(gv, axis=-1, keepdims=True)
    cen = gv - mu
    var = jnp.mean(cen * cen, axis=-1, keepdims=True)
    vn = (cen * lax.rsqrt(var + EPS) * lng_ref[...] + lnb_ref[...]).astype(BF16)
    row = lax.broadcasted_iota(jnp.int32, (CHUNK, CHUNK), 0)
    col = lax.broadcasted_iota(jnp.int32, (CHUNK, CHUNK), 1)
    causal = col <= row
    for g in range(A_GROUPS):
        ws = jnp.where(causal, ws_ref[g], 0.0).astype(BF16)
        cs = slice(g * CHUNK, (g + 1) * CHUNK)
        for c in range(tm // CHUNK):
            rs = slice(c * CHUNK, (c + 1) * CHUNK)
            sv = _dot(ws, vn[rs, cs]) + bs_ref[g]
            y = gu_ref[rs, cs].astype(F32) * sv * sz_ref[rs, cs].astype(F32)
            o_ref[rs, cs] = y.astype(o_ref.dtype)


def _branch_a(proj, ln_g, ln_b, w_s, b_s_b, offs, tm):
    seq = proj.shape[0]
    w = A_GROUPS * CHUNK
    blk = lambda off: pl.BlockSpec((tm, w), lambda i, off=off: (i, off // w))
    const = lambda shape: pl.BlockSpec(shape, lambda i: (0,) * len(shape))
    return pl.pallas_call(
        functools.partial(_branch_a_kernel, tm=tm),
        grid=(seq // tm,),
        in_specs=[blk(offs["u"]), blk(offs["v"]), blk(offs["za"]),
                  const((1, w)), const((1, w)),
                  const((A_GROUPS, CHUNK, CHUNK)), const((A_GROUPS, CHUNK, CHUNK))],
        out_specs=pl.BlockSpec((tm, w), lambda i: (i, 0)),
        out_shape=jax.ShapeDtypeStruct((seq, w), BF16),
        compiler_params=_cparams("parallel"),
        name="branch_a",
    )(proj, proj, proj, ln_g, ln_b, w_s, b_s_b)


def _mla_prep_kernel(cq_ref, ckv_ref, h_ref, wkr_ref, tc_ref, ta_ref, tb_ref, qg_ref, kvg_ref,
                     wq_ref, wkv_ref, q_ref, k_ref, vt_ref):
    scale = QK_DIM ** -0.5 * LOG2E
    tab_c, tab_a, tab_b = tc_ref[...], ta_ref[...], tb_ref[...]
    cqn = _rms(cq_ref[...].astype(F32), qg_ref[...]).astype(BF16)
    ckvn = _rms(ckv_ref[...].astype(F32), kvg_ref[...]).astype(BF16)
    kpe = _rope(_dot_nt(h_ref[...], wkr_ref[...]), tab_c, tab_a, tab_b).astype(BF16)
    for h in range(MLA_HEADS):
        lo, mid, hi = h * HEAD_PAD, h * HEAD_PAD + LANE, (h + 1) * HEAD_PAD
        q = _dot(cqn, wq_ref[:, lo:hi]) * scale
        q_ref[:, lo:mid] = q[:, :LANE].astype(BF16)
        q_ref[:, mid:hi] = _rope(q[:, LANE:], tab_c, tab_a, tab_b).astype(BF16)
        kv = _dot(ckvn, wkv_ref[:, lo:hi])
        k_ref[:, lo:mid] = kv[:, :LANE].astype(BF16)
        k_ref[:, mid:hi] = kpe
        vt_ref[h * V_DIM:(h + 1) * V_DIM, :] = kv[:, LANE:].T.astype(BF16)


def _mla_prep(proj, h, w_kr, tabs, q_norm_g, kv_norm_g, wq_pad, wkv, offs, tm):
    seq, d = h.shape
    hw = MLA_HEADS * HEAD_PAD
    const = lambda shape: pl.BlockSpec(shape, lambda i: (0,) * len(shape))
    col = lambda off, w: pl.BlockSpec((tm, w), lambda i: (i, off // w))
    tab = pl.BlockSpec((tm, LANE), lambda i: (i, 0))
    return pl.pallas_call(
        _mla_prep_kernel,
        grid=(seq // tm,),
        in_specs=[col(offs["cq"], Q_LORA), col(offs["ckv"], KV_LORA),
                  pl.BlockSpec((tm, d), lambda i: (i, 0)), const((LANE, d)),
                  tab, tab, tab, const((1, Q_LORA)), const((1, KV_LORA)),
                  const((Q_LORA, hw)), const((KV_LORA, hw))],
        out_specs=[pl.BlockSpec((tm, hw), lambda i: (i, 0)),
                   pl.BlockSpec((tm, hw), lambda i: (i, 0)),
                   pl.BlockSpec((MLA_HEADS * V_DIM, tm), lambda i: (0, i))],
        out_shape=[jax.ShapeDtypeStruct((seq, hw), BF16),
                   jax.ShapeDtypeStruct((seq, hw), BF16),
                   jax.ShapeDtypeStruct((MLA_HEADS * V_DIM, seq), BF16)],
        compiler_params=_cparams("parallel"),
        name="mla_prep",
    )(proj, proj, h, w_kr, *tabs, q_norm_g, kv_norm_g, wq_pad, wkv)


def _flash_kernel(q_ref, k_ref, vt_ref, g_ref, o_ref, *scratch, tq, tk, cb):
    qi = pl.program_id(1)
    nq = q_ref.shape[0] // tq
    ncb = tq // cb
    base = pl.multiple_of(qi * tq, tq)
    take = lambda n: [scratch[i * ncb:(i + 1) * ncb] for i in range(n)]
    m_refs, l_refs, acc_refs, s0, s1, mn0, mn1, al0, al1 = take(9)
    s_refs, mnew_refs, alpha_refs = (s0, s1), (mn0, mn1), (al0, al1)
    for c in range(ncb):
        l_refs[c][...] = jnp.zeros(l_refs[c].shape, F32)
        acc_refs[c][...] = jnp.zeros(acc_refs[c].shape, F32)

    def scores(c, kstart, rows, slot, mask_off=None, qrow=None):
        qstart = pl.multiple_of((base if qrow is None else qrow) + c * cb, cb)
        s = _dot_nt(k_ref[pl.ds(kstart, rows), :], q_ref[pl.ds(qstart, cb), :])
        if mask_off is not None:
            key = lax.broadcasted_iota(jnp.int32, s.shape, 0)
            qry = lax.broadcasted_iota(jnp.int32, s.shape, 1)
            s = jnp.where(key <= qry + mask_off, s, -1e30)
        m_prev = m_refs[c][...] if qrow is None else jnp.full(m_refs[c].shape, -1e30, F32)
        m_new = jnp.maximum(m_prev, jnp.max(s, axis=0, keepdims=True))
        s_refs[slot][c][:rows, :] = s
        mnew_refs[slot][c][...] = m_new
        alpha_refs[slot][c][...] = jnp.exp2(m_prev - m_new)
        m_refs[c][...] = m_new

    def update(c, kstart, rows, slot):
        alpha = alpha_refs[slot][c][...]
        p = jnp.exp2(s_refs[slot][c][:rows, :] - mnew_refs[slot][c][...])
        l_refs[c][...] = alpha * l_refs[c][...] + jnp.sum(p, axis=0, keepdims=True)
        pv = _dot(vt_ref[:, pl.ds(kstart, rows)], p.astype(BF16))
        acc_refs[c][...] = alpha * acc_refs[c][...] + pv

    def kv(t):
        return pl.multiple_of(t * tk, tk)

    assert tq == 2 * tk

    @pl.when(qi > 0)
    def _():
        def pair(i):
            for c in range(ncb):
                scores(c, kv(2 * i + 1), tk, 1)
                update(c, kv(2 * i), tk, 0)
            for c in range(ncb):
                scores(c, kv(2 * i + 2), tk, 0)
                update(c, kv(2 * i + 1), tk, 1)

        def body(t, carry):
            pair(2 * t)
            pair(2 * t + 1)
            return carry

        lax.fori_loop(0, lax.shift_right_logical(qi - 1, 1), body, 0)

        @pl.when(((qi - 1) & 1) == 1)
        def _():
            pair(qi - 2)

        for c in range(ncb):
            scores(c, kv(2 * qi - 1), tk, 1)
            update(c, kv(2 * qi - 2), tk, 0)

    @pl.when(qi == 0)
    def _():
        for c in range(ncb):
            m_refs[c][...] = jnp.full(m_refs[c].shape, -1e30, F32)
            s_refs[1][c][...] = jnp.zeros(s_refs[1][c].shape, F32)
            mnew_refs[1][c][...] = jnp.full(mnew_refs[1][c].shape, 1e30, F32)
            alpha_refs[1][c][...] = jnp.ones(alpha_refs[1][c].shape, F32)

    pending = kv(jnp.maximum(2 * qi - 1, 0))
    diag = [[], []]
    for d in range(2):
        for c in range(ncb):
            rows = min(tk, (c + 1) * cb - d * tk)
            if rows > 0:
                crosses = d * tk + rows - 1 > c * cb
                diag[d].append(((c, pl.multiple_of(base + d * tk, tk), rows, d),
                                c * cb - d * tk if crosses else None))
    assert [item[0] for item, _ in diag[0]] == list(range(ncb))
    for item, mask_off in diag[0]:
        scores(*item, mask_off)
        update(item[0], pending, tk, 1)
    for n, (item, mask_off) in enumerate(diag[1]):
        scores(*item, mask_off)
        update(*diag[0][n][0])
    rest = [item for item, _ in diag[0][len(diag[1]):] + diag[1]]
    assert len(rest) == ncb and all(rest[c - len(diag[1])][0::3] == (c, 0)
                                    for c in range(len(diag[1]), ncb))
    nxt = pl.multiple_of(jnp.minimum(qi + 1, nq - 1) * tq, tq)
    for c, item in enumerate(rest):
        scores(c, 0, tk, 0, qrow=nxt)
        update(*item)

    for c in range(ncb):
        o = (acc_refs[c][...] / l_refs[c][...]).T
        rs = slice(c * cb, (c + 1) * cb)
        o_ref[rs, :] = (o * g_ref[rs, :].astype(F32)).astype(o_ref.dtype)


def _flash(q, k, vt, proj, zb_off, tq, tk, cb):
    seq = q.shape[0]
    ncb = tq // cb
    stat = [pltpu.VMEM((1, cb), F32)] * ncb
    scratch = (stat + stat + [pltpu.VMEM((V_DIM, cb), F32)] * ncb
               + [pltpu.VMEM((tk, cb), F32)] * (2 * ncb)
               + stat * 4)
    return pl.pallas_call(
        functools.partial(_flash_kernel, tq=tq, tk=tk, cb=cb),
        grid=(MLA_HEADS, seq // tq),
        in_specs=[pl.BlockSpec((seq, HEAD_PAD), lambda h, i: (0, h)),
                  pl.BlockSpec((seq, HEAD_PAD), lambda h, i: (0, h)),
                  pl.BlockSpec((V_DIM, seq), lambda h, i: (h, 0)),
                  pl.BlockSpec((tq, V_DIM), lambda h, i: (i, zb_off // V_DIM + h))],
        out_specs=pl.BlockSpec((tq, V_DIM), lambda h, i: (i, h)),
        out_shape=jax.ShapeDtypeStruct((seq, MLA_HEADS * V_DIM), BF16),
        scratch_shapes=scratch,
        compiler_params=_cparams("arbitrary", "arbitrary"),
        name="flash_mla",
    )(q, k, vt, proj)


def _mem_kv_kernel(mem_ref, g_ref, w_ref, o_ref):
    mn = _rms(mem_ref[...], g_ref[...]).astype(BF16)
    o_ref[...] = _dot(mn, w_ref[...].astype(BF16)).astype(o_ref.dtype)


def _mem_kv(mem2, mem_norm_g, w_mem_kv, tn):
    m, d = mem2.shape
    n = w_mem_kv.shape[1]
    return pl.pallas_call(
        _mem_kv_kernel,
        grid=(n // tn,),
        in_specs=[pl.BlockSpec((m, d), lambda j: (0, 0)),
                  pl.BlockSpec((1, d), lambda j: (0, 0)),
                  pl.BlockSpec((d, tn), lambda j: (0, j))],
        out_specs=pl.BlockSpec((m, tn), lambda j: (0, j)),
        out_shape=jax.ShapeDtypeStruct((m, n), BF16),
        compiler_params=_cparams("parallel"),
        name="mem_kv",
    )(mem2, mem_norm_g, w_mem_kv)


def _mem_attn_kernel(q_ref, kv_ref, sz_ref, o_ref, *, width):
    hd = width // MEM_HEADS
    scale = hd ** -0.5
    for h in range(MEM_HEADS):
        cs = slice(h * hd, (h + 1) * hd)
        s = _dot_nt(q_ref[:, cs], kv_ref[:, cs]) * scale
        p = jnp.exp(s - jnp.max(s, axis=-1, keepdims=True))
        l = jnp.sum(p, axis=-1, keepdims=True)
        o = _dot(p.astype(BF16), kv_ref[:, width + h * hd:width + (h + 1) * hd]) / l
        o_ref[:, cs] = (o * sz_ref[:, cs].astype(F32)).astype(o_ref.dtype)


def _mem_attn(proj, kvm, offs, width, tm):
    seq = proj.shape[0]
    blk = lambda off: pl.BlockSpec((tm, width), lambda i: (i, off // width))
    return pl.pallas_call(
        functools.partial(_mem_attn_kernel, width=width),
        grid=(seq // tm,),
        in_specs=[blk(offs["qm"]), pl.BlockSpec(kvm.shape, lambda i: (0, 0)), blk(offs["zm"])],
        out_specs=pl.BlockSpec((tm, width), lambda i: (i, 0)),
        out_shape=jax.ShapeDtypeStruct((seq, width), BF16),
        compiler_params=_cparams("parallel"),
        name="mem_attn",
    )(proj, kvm, proj)


def _merge_kernel(ya_ref, yb_ref, ym_ref, w_ref, g0_ref, g1_ref, g2_ref, o_ref):
    acc = g0_ref[...].astype(F32) * _dot(ya_ref[...], w_ref[0])
    acc += g1_ref[...].astype(F32) * _dot(yb_ref[...], w_ref[1])
    acc += g2_ref[...].astype(F32) * _dot(ym_ref[...], w_ref[2])
    o_ref[...] = acc.astype(o_ref.dtype)


def _merge(y_a, y_b, y_m, w_branch, proj, gate_off, tm, tn):
    seq, width = y_a.shape
    d = w_branch.shape[2]
    yblk = pl.BlockSpec((tm, width), lambda i, j: (i, 0))
    gblk = lambda n: pl.BlockSpec((tm, tn), lambda i, j, n=n: (i, (gate_off + n * d) // tn + j))
    return pl.pallas_call(
        _merge_kernel,
        grid=(seq // tm, d // tn),
        in_specs=[yblk, yblk, yblk,
                  pl.BlockSpec((N_BRANCH, width, tn), lambda i, j: (0, 0, j)),
                  gblk(0), gblk(1), gblk(2)],
        out_specs=pl.BlockSpec((tm, tn), lambda i, j: (i, j)),
        out_shape=jax.ShapeDtypeStruct((seq, d), BF16),
        compiler_params=_cparams("parallel", "arbitrary"),
        name="merge",
    )(y_a, y_b, y_m, w_branch, proj, proj, proj)


def _out_kernel(m_ref, w_ref, x_ref, g_ref, o_ref):
    out = _dot(m_ref[...], w_ref[...])
    o_ref[...] = x_ref[...] + _rms(out, g_ref[...])


def _out_proj(merged, w_out, x2, g_post, tm):
    seq, d = x2.shape
    return pl.pallas_call(
        _out_kernel,
        grid=(seq // tm,),
        in_specs=[pl.BlockSpec((tm, d), lambda i: (i, 0)),
                  pl.BlockSpec((d, d), lambda i: (0, 0)),
                  pl.BlockSpec((tm, d), lambda i: (i, 0)),
                  pl.BlockSpec((1, d), lambda i: (0, 0))],
        out_specs=pl.BlockSpec((tm, d), lambda i: (i, 0)),
        out_shape=jax.ShapeDtypeStruct((seq, d), F32),
        compiler_params=_cparams("parallel"),
        name="out_proj",
    )(merged, w_out, x2, g_post)


def _layer(x2, mem2, tabs, g_pre, w_in, a_ln_g, a_ln_b, a_w_s, a_b_s, q_norm_g, w_uq,
           kv_norm_g, w_ukv, mem_norm_g, w_mem_kv, w_gate, b_gate, w_branch, w_out, g_post):
    seq, d = x2.shape
    tn = 1024
    w_in_t = jnp.swapaxes(w_in, 0, 1).astype(BF16)
    lat, tail = 3 * d, 3 * d + Q_LORA + KV_LORA
    w_kr = jnp.pad(w_in_t[tail:tail + QK_ROPE], ((0, LANE - QK_ROPE), (0, 0)))
    sections = (("u", _gelu, 0, d), ("v", _gelu, d, d), ("za", _silu, 2 * d, d),
                ("zb", _silu, tail + QK_ROPE, d), ("qm", _identity, tail + QK_ROPE + d, d),
                ("zm", _silu, tail + QK_ROPE + 2 * d, d), ("gate", _sigmoid, None, N_BRANCH * d),
                ("cq", _identity, lat, Q_LORA + KV_LORA))
    offs, tiles, col = {}, [], 0
    for name, act, row, width in sections:
        offs[name] = col
        tiles += [(act, None if row is None else row + t) for t in range(0, width, tn)]
        col += width
    offs["ckv"] = offs["cq"] + Q_LORA

    proj, h = _projection(x2, g_pre.reshape(1, d), w_in_t, w_gate.astype(BF16),
                          b_gate.reshape(1, -1), tuple(tiles), tm=1024, tn=tn)

    b_s_b = jnp.broadcast_to(a_b_s[:, :, None], (A_GROUPS, CHUNK, CHUNK))
    y_a = _branch_a(proj, a_ln_g.reshape(1, d), a_ln_b.reshape(1, d), a_w_s, b_s_b, offs, tm=512)

    wq_pad = jnp.pad(w_uq.reshape(Q_LORA, MLA_HEADS, QK_DIM),
                     ((0, 0), (0, 0), (0, HEAD_PAD - QK_DIM))).reshape(Q_LORA, -1).astype(BF16)
    q, k, vv = _mla_prep(proj, h, w_kr, tabs, q_norm_g.reshape(1, -1), kv_norm_g.reshape(1, -1),
                         wq_pad, w_ukv.astype(BF16), offs, tm=512)
    y_b = _flash(q, k, vv, proj, offs["zb"], tq=1024, tk=512, cb=256)

    kvm = _mem_kv(mem2, mem_norm_g.reshape(1, d), w_mem_kv, tn=512)
    y_m = _mem_attn(proj, kvm, offs, d, tm=1024)

    merged = _merge(y_a, y_b, y_m, w_branch.astype(BF16), proj, offs["gate"], tm=1024, tn=512)
    return _out_proj(merged, w_out.astype(BF16), x2, g_post.reshape(1, d), tm=512)


def kernel(x, mem, positions, g_pre, w_in, a_ln_g, a_ln_b, a_w_s, a_b_s, q_norm_g, w_uq,
           kv_norm_g, w_ukv, mem_norm_g, w_mem_kv, w_gate, b_gate, w_branch, w_out, g_post):
    batch, seq, d = x.shape
    depth = g_pre.shape[0]
    outs = []
    for b in range(batch):
        tabs = _rope_tables(positions[b], seq)
        xb = x[b]
        for l in range(depth):
            xb = _layer(xb, mem[b], tabs, g_pre[l], w_in[l], a_ln_g[l], a_ln_b[l], a_w_s[l],
                        a_b_s[l], q_norm_g[l], w_uq[l], kv_norm_g[l], w_ukv[l], mem_norm_g[l],
                        w_mem_kv[l], w_gate[l], b_gate[l], w_branch[l], w_out[l], g_post[l])
        outs.append(xb)
    return jnp.stack(outs, axis=0)
```

```python
import functools

import jax
import jax.numpy as jnp
from jax import lax
from jax.experimental import pallas as pl
from jax.experimental.pallas import tpu as pltpu

EPS = 1e-6
CHUNK = 128
A_GROUPS = 16
QK_NOPE = 128
QK_ROPE = 64
V_DIM = 128
MLA_HEADS = 16
Q_LORA = 512
KV_LORA = 512
QK_DIM = QK_NOPE + QK_ROPE
ROPE_THETA = 10000.0
LOG2E = 1.4426950408889634
GELU_C = 0.7978845608028654
PROJ_CHUNK = 256
MEM_HEADS = 4
N_BRANCH = 3
LANE = 128
HEAD_PAD = 256
VMEM_LIMIT = 52 * 1024 * 1024

F32 = jnp.float32
BF16 = jnp.bfloat16


def _cparams(*sem):
    return pltpu.CompilerParams(dimension_semantics=sem, vmem_limit_bytes=VMEM_LIMIT)


def _dot(a, b):
    return jnp.dot(a, b, preferred_element_type=F32)


def _dot_nt(a, b):
    return lax.dot_general(a, b, (((1,), (1,)), ((), ())), preferred_element_type=F32)


def _gelu(x):
    half = 0.5 * x
    return half + half * jnp.tanh(x * (GELU_C + (GELU_C * 0.044715) * (x * x)))


def _sigmoid(x):
    return 0.5 + 0.5 * jnp.tanh(0.5 * x)


def _silu(x):
    half = 0.5 * x
    return half + half * jnp.tanh(half)


def _identity(x):
    return x


def _rms(xf, g):
    return xf * lax.rsqrt(jnp.mean(xf * xf, axis=-1, keepdims=True) + EPS) * g


def _rope_table_kernel(pos_ref, invf_ref, tc_ref, ta_ref, tb_ref):
    ang = pos_ref[...].astype(F32) * invf_ref[...]
    cos, sin = jnp.cos(ang), jnp.sin(ang)
    z32 = jnp.zeros_like(cos)
    z64 = jnp.concatenate([z32, z32], axis=0)
    tc_ref[...] = jnp.concatenate([cos, cos, z64], axis=0).T
    ta_ref[...] = jnp.concatenate([sin, z32, z64], axis=0).T
    tb_ref[...] = jnp.concatenate([z32, sin, z64], axis=0).T


def _rope_tables(positions, seq):
    half = QK_ROPE // 2
    inv_freq = 1.0 / (ROPE_THETA ** (jnp.arange(0, QK_ROPE, 2, dtype=F32) / QK_ROPE))
    return pl.pallas_call(
        _rope_table_kernel,
        out_shape=(jax.ShapeDtypeStruct((seq, LANE), F32),) * 3,
        compiler_params=pltpu.CompilerParams(vmem_limit_bytes=VMEM_LIMIT),
        name="rope_tables",
    )(positions.reshape(1, seq), inv_freq.reshape(half, 1))


def _rope(t, tab_c, tab_a, tab_b):
    down = pltpu.roll(t, 3 * LANE // 4, axis=1)
    up = pltpu.roll(t, LANE // 4, axis=1)
    return t * tab_c - down * tab_a + up * tab_b


def _proj_kernel(x_ref, g_ref, wt_ref, wg_ref, b_ref, wkr_ref, o_ref, kr_ref, h_ref, *, tiles):
    j = pl.program_id(1)

    def tile(act, is_gate):
        for lo in range(0, o_ref.shape[1], PROJ_CHUNK):
            cs = slice(lo, lo + PROJ_CHUNK)
            if is_gate:
                acc = _dot(h_ref[...], wg_ref[:, cs]) + b_ref[:, cs]
            else:
                acc = _dot_nt(h_ref[...], wt_ref[cs, :])
            o_ref[:, cs] = act(acc).astype(o_ref.dtype)

    kind = lambda t: (tiles[t][0], tiles[t][1] is None)

    @pl.when(j == 0)
    def _():
        h_ref[...] = _rms(x_ref[...], g_ref[...]).astype(BF16)
        tile(*kind(0))

    later = range(1, len(tiles))
    for k in sorted({kind(t) for t in later}, key=lambda k: (k[0].__name__, k[1])):
        @pl.when(functools.reduce(jnp.logical_or, [j == t for t in later if kind(t) == k]))
        def _(k=k):
            tile(*k)

    @pl.when(j == len(tiles) - 1)
    def _():
        kr_ref[...] = _dot_nt(h_ref[...], wkr_ref[...]).astype(kr_ref.dtype)


def _projection(x2, g_pre, w_in_t, w_gate, b_gate, w_kr, tiles, tm, tn):
    seq, d = x2.shape
    n_tiles = len(tiles)

    def pick(j, values):
        return functools.reduce(lambda acc, tv: jnp.where(j == tv[0], tv[1], acc),
                                enumerate(values), values[0])

    rows, gates, row, gate = [], [], 0, 0
    for act, src in tiles:
        if src is None:
            gates.append(gate)
            gate += 1
        else:
            row = src
            gates.append(max(gate - 1, 0))
        rows.append(row)
    return pl.pallas_call(
        functools.partial(_proj_kernel, tiles=tiles),
        grid=(seq // tm, n_tiles),
        in_specs=[
            pl.BlockSpec((tm, d), lambda i, j: (i, 0)),
            pl.BlockSpec((1, d), lambda i, j: (0, 0)),
            pl.BlockSpec((pl.Element(tn), pl.Element(d)),
                         lambda i, j: (pick(j, [r // QK_ROPE for r in rows]) * QK_ROPE, 0)),
            pl.BlockSpec((d, tn), lambda i, j: (0, pick(j, gates))),
            pl.BlockSpec((1, tn), lambda i, j: (0, pick(j, gates))),
            pl.BlockSpec((LANE, d), lambda i, j: (0, 0)),
        ],
        out_specs=[pl.BlockSpec((tm, tn), lambda i, j: (i, j)),
                   pl.BlockSpec((tm, LANE), lambda i, j: (i, 0))],
        out_shape=[jax.ShapeDtypeStruct((seq, n_tiles * tn), BF16),
                   jax.ShapeDtypeStruct((seq, LANE), BF16)],
        scratch_shapes=[pltpu.VMEM((tm, d), BF16)],
        compiler_params=_cparams("parallel", "arbitrary"),
        name="in_proj",
    )(x2, g_pre, w_in_t, w_gate, b_gate, w_kr)


def _branch_a_kernel(gu_ref, gv_ref, sz_ref, lng_ref, lnb_ref, ws_ref, bs_ref, o_ref, *, tm):
    gv = gv_ref[...].astype(F32)
    mu = jnp.mean(gv, axis=-1, keepdims=True)
    cen = gv - mu
    var = jnp.mean(cen * cen, axis=-1, keepdims=True)
    vn = (cen * lax.rsqrt(var + EPS) * lng_ref[...] + lnb_ref[...]).astype(BF16)
    row = lax.broadcasted_iota(jnp.int32, (CHUNK, CHUNK), 0)
    col = lax.broadcasted_iota(jnp.int32, (CHUNK, CHUNK), 1)
    causal = col <= row
    for g in range(A_GROUPS):
        ws = jnp.where(causal, ws_ref[g], 0.0).astype(BF16)
        cs = slice(g * CHUNK, (g + 1) * CHUNK)
        for c in range(tm // CHUNK):
            rs = slice(c * CHUNK, (c + 1) * CHUNK)
            sv = _dot(ws, vn[rs, cs]) + bs_ref[g]
            y = gu_ref[rs, cs].astype(F32) * sv * sz_ref[rs, cs].astype(F32)
            o_ref[rs, cs] = y.astype(o_ref.dtype)


def _branch_a(proj, ln_g, ln_b, w_s, b_s_b, offs, tm):
    seq = proj.shape[0]
    w = A_GROUPS * CHUNK
    blk = lambda off: pl.BlockSpec((tm, w), lambda i, off=off: (i, off // w))
    const = lambda shape: pl.BlockSpec(shape, lambda i: (0,) * len(shape))
    return pl.pallas_call(
        functools.partial(_branch_a_kernel, tm=tm),
        grid=(seq // tm,),
        in_specs=[blk(offs["u"]), blk(offs["v"]), blk(offs["za"]),
                  const((1, w)), const((1, w)),
                  const((A_GROUPS, CHUNK, CHUNK)), const((A_GROUPS, CHUNK, CHUNK))],
        out_specs=pl.BlockSpec((tm, w), lambda i: (i, 0)),
        out_shape=jax.ShapeDtypeStruct((seq, w), BF16),
        compiler_params=_cparams("parallel"),
        name="branch_a",
    )(proj, proj, proj, ln_g, ln_b, w_s, b_s_b)


def _mla_prep_kernel(cq_ref, ckv_ref, kr_ref, tc_ref, ta_ref, tb_ref, qg_ref, kvg_ref,
                     wq_ref, wkv_ref, q_ref, k_ref, vt_ref):
    scale = QK_DIM ** -0.5 * LOG2E
    tab_c, tab_a, tab_b = tc_ref[...], ta_ref[...], tb_ref[...]
    cqn = _rms(cq_ref[...].astype(F32), qg_ref[...]).astype(BF16)
    ckvn = _rms(ckv_ref[...].astype(F32), kvg_ref[...]).astype(BF16)
    kpe = _rope(kr_ref[...].astype(F32), tab_c, tab_a, tab_b).astype(BF16)
    for h in range(MLA_HEADS):
        lo, mid, hi = h * HEAD_PAD, h * HEAD_PAD + LANE, (h + 1) * HEAD_PAD
        q = _dot(cqn, wq_ref[:, lo:hi]) * scale
        q_ref[:, lo:mid] = q[:, :LANE].astype(BF16)
        q_ref[:, mid:hi] = _rope(q[:, LANE:], tab_c, tab_a, tab_b).astype(BF16)
        kv = _dot(ckvn, wkv_ref[:, lo:hi])
        k_ref[:, lo:mid] = kv[:, :LANE].astype(BF16)
        k_ref[:, mid:hi] = kpe
        vt_ref[h * V_DIM:(h + 1) * V_DIM, :] = kv[:, LANE:].T.astype(BF16)


def _mla_prep(proj, kr, tabs, q_norm_g, kv_norm_g, wq_pad, wkv, offs, tm):
    seq = proj.shape[0]
    hw = MLA_HEADS * HEAD_PAD
    const = lambda shape: pl.BlockSpec(shape, lambda i: (0,) * len(shape))
    col = lambda off, w: pl.BlockSpec((tm, w), lambda i: (i, off // w))
    tab = pl.BlockSpec((tm, LANE), lambda i: (i, 0))
    return pl.pallas_call(
        _mla_prep_kernel,
        grid=(seq // tm,),
        in_specs=[col(offs["cq"], Q_LORA), col(offs["ckv"], KV_LORA),
                  tab, tab, tab, tab, const((1, Q_LORA)), const((1, KV_LORA)),
                  const((Q_LORA, hw)), const((KV_LORA, hw))],
        out_specs=[pl.BlockSpec((tm, hw), lambda i: (i, 0)),
                   pl.BlockSpec((tm, hw), lambda i: (i, 0)),
                   pl.BlockSpec((MLA_HEADS * V_DIM, tm), lambda i: (0, i))],
        out_shape=[jax.ShapeDtypeStruct((seq, hw), BF16),
                   jax.ShapeDtypeStruct((seq, hw), BF16),
                   jax.ShapeDtypeStruct((MLA_HEADS * V_DIM, seq), BF16)],
        compiler_params=_cparams("parallel"),
        name="mla_prep",
    )(proj, proj, kr, *tabs, q_norm_g, kv_norm_g, wq_pad, wkv)


def _flash_kernel(q_ref, k_ref, vt_ref, g_ref, o_ref, *scratch, tq, tk, cb):
    qi = pl.program_id(1)
    nq = q_ref.shape[0] // tq
    ncb = tq // cb
    base = pl.multiple_of(qi * tq, tq)
    take = lambda n: [scratch[i * ncb:(i + 1) * ncb] for i in range(n)]
    m_refs, l_refs, acc_refs, s0, s1, mn0, mn1, al0, al1 = take(9)
    s_refs, mnew_refs, alpha_refs = (s0, s1), (mn0, mn1), (al0, al1)
    for c in range(ncb):
        l_refs[c][...] = jnp.zeros(l_refs[c].shape, F32)
        acc_refs[c][...] = jnp.zeros(acc_refs[c].shape, F32)

    def scores(c, kstart, rows, slot, mask_off=None, qrow=None):
        qstart = pl.multiple_of((base if qrow is None else qrow) + c * cb, cb)
        s = _dot_nt(k_ref[pl.ds(kstart, rows), :], q_ref[pl.ds(qstart, cb), :])
        if mask_off is not None:
            key = lax.broadcasted_iota(jnp.int32, s.shape, 0)
            qry = lax.broadcasted_iota(jnp.int32, s.shape, 1)
            s = jnp.where(key <= qry + mask_off, s, -1e30)
        m_prev = m_refs[c][...] if qrow is None else jnp.full(m_refs[c].shape, -1e30, F32)
        m_new = jnp.maximum(m_prev, jnp.max(s, axis=0, keepdims=True))
        s_refs[slot][c][:rows, :] = s
        mnew_refs[slot][c][...] = m_new
        alpha_refs[slot][c][...] = jnp.exp2(m_prev - m_new)
        m_refs[c][...] = m_new

    def update(c, kstart, rows, slot):
        alpha = alpha_refs[slot][c][...]
        p = jnp.exp2(s_refs[slot][c][:rows, :] - mnew_refs[slot][c][...])
        l_refs[c][...] = alpha * l_refs[c][...] + jnp.sum(p, axis=0, keepdims=True)
        pv = _dot(vt_ref[:, pl.ds(kstart, rows)], p.astype(BF16))
        acc_refs[c][...] = alpha * acc_refs[c][...] + pv

    def kv(t):
        return pl.multiple_of(t * tk, tk)

    assert tq == 2 * tk

    @pl.when(qi > 0)
    def _():
        def pair(i):
            for c in range(ncb):
                scores(c, kv(2 * i + 1), tk, 1)
                update(c, kv(2 * i), tk, 0)
            for c in range(ncb):
                scores(c, kv(2 * i + 2), tk, 0)
                update(c, kv(2 * i + 1), tk, 1)

        def body(t, carry):
            pair(2 * t)
            pair(2 * t + 1)
            return carry

        lax.fori_loop(0, lax.shift_right_logical(qi - 1, 1), body, 0)

        @pl.when(((qi - 1) & 1) == 1)
        def _():
            pair(qi - 2)

        for c in range(ncb):
            scores(c, kv(2 * qi - 1), tk, 1)
            update(c, kv(2 * qi - 2), tk, 0)

    @pl.when(qi == 0)
    def _():
        for c in range(ncb):
            m_refs[c][...] = jnp.full(m_refs[c].shape, -1e30, F32)
            s_refs[1][c][...] = jnp.zeros(s_refs[1][c].shape, F32)
            mnew_refs[1][c][...] = jnp.full(mnew_refs[1][c].shape, 1e30, F32)
            alpha_refs[1][c][...] = jnp.ones(alpha_refs[1][c].shape, F32)

    pending = kv(jnp.maximum(2 * qi - 1, 0))
    diag = [[], []]
    for d in range(2):
        for c in range(ncb):
            rows = min(tk, (c + 1) * cb - d * tk)
            if rows > 0:
                crosses = d * tk + rows - 1 > c * cb
                diag[d].append(((c, pl.multiple_of(base + d * tk, tk), rows, d),
                                c * cb - d * tk if crosses else None))
    assert [item[0] for item, _ in diag[0]] == list(range(ncb))
    for item, mask_off in diag[0]:
        scores(*item, mask_off)
        update(item[0], pending, tk, 1)
    for n, (item, mask_off) in enumerate(diag[1]):
        scores(*item, mask_off)
        update(*diag[0][n][0])
    rest = [item for item, _ in diag[0][len(diag[1]):] + diag[1]]
    assert len(rest) == ncb and all(rest[c - len(diag[1])][0::3] == (c, 0)
                                    for c in range(len(diag[1]), ncb))
    nxt = pl.multiple_of(jnp.minimum(qi + 1, nq - 1) * tq, tq)
    for c, item in enumerate(rest):
        scores(c, 0, tk, 0, qrow=nxt)
        update(*item)

    for c in range(ncb):
        o = (acc_refs[c][...] / l_refs[c][...]).T
        rs = slice(c * cb, (c + 1) * cb)
        o_ref[rs, :] = (o * g_ref[rs, :].astype(F32)).astype(o_ref.dtype)


def _flash(q, k, vt, proj, zb_off, tq, tk, cb):
    seq = q.shape[0]
    ncb = tq // cb
    stat = [pltpu.VMEM((1, cb), F32)] * ncb
    scratch = (stat + stat + [pltpu.VMEM((V_DIM, cb), F32)] * ncb
               + [pltpu.VMEM((tk, cb), F32)] * (2 * ncb)
               + stat * 4)
    return pl.pallas_call(
        functools.partial(_flash_kernel, tq=tq, tk=tk, cb=cb),
        grid=(MLA_HEADS, seq // tq),
        in_specs=[pl.BlockSpec((seq, HEAD_PAD), lambda h, i: (0, h)),
                  pl.BlockSpec((seq, HEAD_PAD), lambda h, i: (0, h)),
                  pl.BlockSpec((V_DIM, seq), lambda h, i: (h, 0)),
                  pl.BlockSpec((tq, V_DIM), lambda h, i: (i, zb_off // V_DIM + h))],
        out_specs=pl.BlockSpec((tq, V_DIM), lambda h, i: (i, h)),
        out_shape=jax.ShapeDtypeStruct((seq, MLA_HEADS * V_DIM), BF16),
        scratch_shapes=scratch,
        compiler_params=_cparams("arbitrary", "arbitrary"),
        name="flash_mla",
    )(q, k, vt, proj)


def _mem_kv_kernel(mem_ref, g_ref, w_ref, o_ref):
    mn = _rms(mem_ref[...], g_ref[...]).astype(BF16)
    o_ref[...] = _dot(mn, w_ref[...].astype(BF16)).astype(o_ref.dtype)


def _mem_kv(mem2, mem_norm_g, w_mem_kv, tn):
    m, d = mem2.shape
    n = w_mem_kv.shape[1]
    return pl.pallas_call(
        _mem_kv_kernel,
        grid=(n // tn,),
        in_specs=[pl.BlockSpec((m, d), lambda j: (0, 0)),
                  pl.BlockSpec((1, d), lambda j: (0, 0)),
                  pl.BlockSpec((d, tn), lambda j: (0, j))],
        out_specs=pl.BlockSpec((m, tn), lambda j: (0, j)),
        out_shape=jax.ShapeDtypeStruct((m, n), BF16),
        compiler_params=_cparams("parallel"),
        name="mem_kv",
    )(mem2, mem_norm_g, w_mem_kv)


def _mem_attn_kernel(q_ref, kv_ref, sz_ref, o_ref, *, width):
    hd = width // MEM_HEADS
    scale = hd ** -0.5
    for h in range(MEM_HEADS):
        cs = slice(h * hd, (h + 1) * hd)
        s = _dot_nt(q_ref[:, cs], kv_ref[:, cs]) * scale
        p = jnp.exp(s - jnp.max(s, axis=-1, keepdims=True))
        l = jnp.sum(p, axis=-1, keepdims=True)
        o = _dot(p.astype(BF16), kv_ref[:, width + h * hd:width + (h + 1) * hd]) / l
        o_ref[:, cs] = (o * sz_ref[:, cs].astype(F32)).astype(o_ref.dtype)


def _mem_attn(proj, kvm, offs, width, tm):
    seq = proj.shape[0]
    blk = lambda off: pl.BlockSpec((tm, width), lambda i: (i, off // width))
    return pl.pallas_call(
        functools.partial(_mem_attn_kernel, width=width),
        grid=(seq // tm,),
        in_specs=[blk(offs["qm"]), pl.BlockSpec(kvm.shape, lambda i: (0, 0)), blk(offs["zm"])],
        out_specs=pl.BlockSpec((tm, width), lambda i: (i, 0)),
        out_shape=jax.ShapeDtypeStruct((seq, width), BF16),
        compiler_params=_cparams("parallel"),
        name="mem_attn",
    )(proj, kvm, proj)


def _merge_kernel(ya_ref, yb_ref, ym_ref, w_ref, g0_ref, g1_ref, g2_ref, o_ref):
    acc = g0_ref[...].astype(F32) * _dot(ya_ref[...], w_ref[0])
    acc += g1_ref[...].astype(F32) * _dot(yb_ref[...], w_ref[1])
    acc += g2_ref[...].astype(F32) * _dot(ym_ref[...], w_ref[2])
    o_ref[...] = acc.astype(o_ref.dtype)


def _merge(y_a, y_b, y_m, w_branch, proj, gate_off, tm, tn):
    seq, width = y_a.shape
    d = w_branch.shape[2]
    yblk = pl.BlockSpec((tm, width), lambda i, j: (i, 0))
    gblk = lambda n: pl.BlockSpec((tm, tn), lambda i, j, n=n: (i, (gate_off + n * d) // tn + j))
    return pl.pallas_call(
        _merge_kernel,
        grid=(seq // tm, d // tn),
        in_specs=[yblk, yblk, yblk,
                  pl.BlockSpec((N_BRANCH, width, tn), lambda i, j: (0, 0, j)),
                  gblk(0), gblk(1), gblk(2)],
        out_specs=pl.BlockSpec((tm, tn), lambda i, j: (i, j)),
        out_shape=jax.ShapeDtypeStruct((seq, d), BF16),
        compiler_params=_cparams("parallel", "arbitrary"),
        name="merge",
    )(y_a, y_b, y_m, w_branch, proj, proj, proj)


def _out_kernel(m_ref, w_ref, x_ref, g_ref, o_ref):
    out = _dot(m_ref[...], w_ref[...])
    o_ref[...] = x_ref[...] + _rms(out, g_ref[...])


def _out_proj(merged, w_out, x2, g_post, tm):
    seq, d = x2.shape
    return pl.pallas_call(
        _out_kernel,
        grid=(seq // tm,),
        in_specs=[pl.BlockSpec((tm, d), lambda i: (i, 0)),
                  pl.BlockSpec((d, d), lambda i: (0, 0)),
                  pl.BlockSpec((tm, d), lambda i: (i, 0)),
                  pl.BlockSpec((1, d), lambda i: (0, 0))],
        out_specs=pl.BlockSpec((tm, d), lambda i: (i, 0)),
        out_shape=jax.ShapeDtypeStruct((seq, d), F32),
        compiler_params=_cparams("parallel"),
        name="out_proj",
    )(merged, w_out, x2, g_post)


def _layer(x2, mem2, tabs, g_pre, w_in, a_ln_g, a_ln_b, a_w_s, a_b_s, q_norm_g, w_uq,
           kv_norm_g, w_ukv, mem_norm_g, w_mem_kv, w_gate, b_gate, w_branch, w_out, g_post):
    seq, d = x2.shape
    tn = 1024
    w_in_t = jnp.swapaxes(w_in, 0, 1).astype(BF16)
    lat, tail = 3 * d, 3 * d + Q_LORA + KV_LORA
    w_kr = jnp.pad(w_in_t[tail:tail + QK_ROPE], ((0, LANE - QK_ROPE), (0, 0)))
    sections = (("u", _gelu, 0, d), ("v", _gelu, d, d), ("za", _silu, 2 * d, d),
                ("zb", _silu, tail + QK_ROPE, d), ("qm", _identity, tail + QK_ROPE + d, d),
                ("zm", _silu, tail + QK_ROPE + 2 * d, d), ("gate", _sigmoid, None, N_BRANCH * d),
                ("cq", _identity, lat, Q_LORA + KV_LORA))
    offs, tiles, col = {}, [], 0
    for name, act, row, width in sections:
        offs[name] = col
        tiles += [(act, None if row is None else row + t) for t in range(0, width, tn)]
        col += width
    offs["ckv"] = offs["cq"] + Q_LORA

    proj, kr = _projection(x2, g_pre.reshape(1, d), w_in_t, w_gate.astype(BF16),
                           b_gate.reshape(1, -1), w_kr, tuple(tiles), tm=1024, tn=tn)

    b_s_b = jnp.broadcast_to(a_b_s[:, :, None], (A_GROUPS, CHUNK, CHUNK))
    y_a = _branch_a(proj, a_ln_g.reshape(1, d), a_ln_b.reshape(1, d), a_w_s, b_s_b, offs, tm=512)

    wq_pad = jnp.pad(w_uq.reshape(Q_LORA, MLA_HEADS, QK_DIM),
                     ((0, 0), (0, 0), (0, HEAD_PAD - QK_DIM))).reshape(Q_LORA, -1).astype(BF16)
    q, k, vv = _mla_prep(proj, kr, tabs, q_norm_g.reshape(1, -1), kv_norm_g.reshape(1, -1),
                         wq_pad, w_ukv.astype(BF16), offs, tm=512)
    y_b = _flash(q, k, vv, proj, offs["zb"], tq=1024, tk=512, cb=256)

    kvm = _mem_kv(mem2, mem_norm_g.reshape(1, d), w_mem_kv, tn=512)
    y_m = _mem_attn(proj, kvm, offs, d, tm=1024)

    merged = _merge(y_a, y_b, y_m, w_branch.astype(BF16), proj, offs["gate"], tm=1024, tn=512)
    return _out_proj(merged, w_out.astype(BF16), x2, g_post.reshape(1, d), tm=512)


def kernel(x, mem, positions, g_pre, w_in, a_ln_g, a_ln_b, a_w_s, a_b_s, q_norm_g, w_uq,
           kv_norm_g, w_ukv, mem_norm_g, w_mem_kv, w_gate, b_gate, w_branch, w_out, g_post):
    batch, seq, d = x.shape
    depth = g_pre.shape[0]
    outs = []
    for b in range(batch):
        tabs = _rope_tables(positions[b], seq)
        xb = x[b]
        for l in range(depth):
            xb = _layer(xb, mem[b], tabs, g_pre[l], w_in[l], a_ln_g[l], a_ln_b[l], a_w_s[l],
                        a_b_s[l], q_norm_g[l], w_uq[l], kv_norm_g[l], w_ukv[l], mem_norm_g[l],
                        w_mem_kv[l], w_gate[l], b_gate[l], w_branch[l], w_out[l], g_post[l])
        outs.append(xb)
    return jnp.stack(outs, axis=0)
```

```python
import functools

import jax
import jax.numpy as jnp
from jax import lax
from jax.experimental import pallas as pl
from jax.experimental.pallas import tpu as pltpu

EPS = 1e-6
CHUNK = 128
A_GROUPS = 16
QK_NOPE = 128
QK_ROPE = 64
V_DIM = 128
MLA_HEADS = 16
Q_LORA = 512
KV_LORA = 512
QK_DIM = QK_NOPE + QK_ROPE
ROPE_THETA = 10000.0
LOG2E = 1.4426950408889634
GELU_C = 0.7978845608028654
PROJ_CHUNK = 256
MEM_HEADS = 4
N_BRANCH = 3
LANE = 128
HEAD_PAD = 256
VMEM_LIMIT = 52 * 1024 * 1024

F32 = jnp.float32
BF16 = jnp.bfloat16


def _cparams(*sem):
    return pltpu.CompilerParams(dimension_semantics=sem, vmem_limit_bytes=VMEM_LIMIT)


def _dot(a, b):
    return jnp.dot(a, b, preferred_element_type=F32)


def _dot_nt(a, b):
    return lax.dot_general(a, b, (((1,), (1,)), ((), ())), preferred_element_type=F32)


def _gelu(x):
    half = 0.5 * x
    return half + half * jnp.tanh(x * (GELU_C + (GELU_C * 0.044715) * (x * x)))


def _sigmoid(x):
    return 0.5 + 0.5 * jnp.tanh(0.5 * x)


def _silu(x):
    half = 0.5 * x
    return half + half * jnp.tanh(half)


def _identity(x):
    return x


def _rms(xf, g):
    return xf * lax.rsqrt(jnp.mean(xf * xf, axis=-1, keepdims=True) + EPS) * g


def _rope_table_kernel(pos_ref, invf_ref, tc_ref, ta_ref, tb_ref):
    ang = pos_ref[...].astype(F32) * invf_ref[...]
    cos, sin = jnp.cos(ang), jnp.sin(ang)
    z32 = jnp.zeros_like(cos)
    z64 = jnp.concatenate([z32, z32], axis=0)
    tc_ref[...] = jnp.concatenate([cos, cos, z64], axis=0).T
    ta_ref[...] = jnp.concatenate([sin, z32, z64], axis=0).T
    tb_ref[...] = jnp.concatenate([z32, sin, z64], axis=0).T


def _rope_tables(positions, seq):
    half = QK_ROPE // 2
    inv_freq = 1.0 / (ROPE_THETA ** (jnp.arange(0, QK_ROPE, 2, dtype=F32) / QK_ROPE))
    return pl.pallas_call(
        _rope_table_kernel,
        out_shape=(jax.ShapeDtypeStruct((seq, LANE), F32),) * 3,
        compiler_params=pltpu.CompilerParams(vmem_limit_bytes=VMEM_LIMIT),
        name="rope_tables",
    )(positions.reshape(1, seq), inv_freq.reshape(half, 1))


def _rope(t, tab_c, tab_a, tab_b):
    down = pltpu.roll(t, 3 * LANE // 4, axis=1)
    up = pltpu.roll(t, LANE // 4, axis=1)
    return t * tab_c - down * tab_a + up * tab_b


def _proj_kernel(x_ref, g_ref, wt_ref, wg_ref, b_ref, wkr_ref, o_ref, kr_ref, h_ref, *, tiles):
    j = pl.program_id(1)

    def tile(act, is_gate):
        for lo in range(0, o_ref.shape[1], PROJ_CHUNK):
            cs = slice(lo, lo + PROJ_CHUNK)
            if is_gate:
                acc = _dot(h_ref[...], wg_ref[:, cs]) + b_ref[:, cs]
            else:
                acc = _dot_nt(h_ref[...], wt_ref[cs, :])
            o_ref[:, cs] = act(acc).astype(o_ref.dtype)

    kind = lambda t: (tiles[t][0], tiles[t][1] is None)

    @pl.when(j == 0)
    def _():
        h_ref[...] = _rms(x_ref[...], g_ref[...]).astype(BF16)
        tile(*kind(0))

    later = range(1, len(tiles))
    for k in sorted({kind(t) for t in later}, key=lambda k: (k[0].__name__, k[1])):
        @pl.when(functools.reduce(jnp.logical_or, [j == t for t in later if kind(t) == k]))
        def _(k=k):
            tile(*k)

    @pl.when(j == len(tiles) - 1)
    def _():
        kr_ref[...] = _dot_nt(h_ref[...], wkr_ref[...]).astype(kr_ref.dtype)


def _projection(x2, g_pre, w_in_t, w_gate, b_gate, w_kr, tiles, tm, tn):
    seq, d = x2.shape
    n_tiles = len(tiles)

    def pick(j, values):
        return functools.reduce(lambda acc, tv: jnp.where(j == tv[0], tv[1], acc),
                                enumerate(values), values[0])

    rows, gates, row, gate = [], [], 0, 0
    for act, src in tiles:
        if src is None:
            gates.append(gate)
            gate += 1
        else:
            row = src
            gates.append(max(gate - 1, 0))
        rows.append(row)
    return pl.pallas_call(
        functools.partial(_proj_kernel, tiles=tiles),
        grid=(seq // tm, n_tiles),
        in_specs=[
            pl.BlockSpec((tm, d), lambda i, j: (i, 0)),
            pl.BlockSpec((1, d), lambda i, j: (0, 0)),
            pl.BlockSpec((pl.Element(tn), pl.Element(d)),
                         lambda i, j: (pick(j, [r // QK_ROPE for r in rows]) * QK_ROPE, 0)),
            pl.BlockSpec((d, tn), lambda i, j: (0, pick(j, gates))),
            pl.BlockSpec((1, tn), lambda i, j: (0, pick(j, gates))),
            pl.BlockSpec((LANE, d), lambda i, j: (0, 0)),
        ],
        out_specs=[pl.BlockSpec((tm, tn), lambda i, j: (i, j)),
                   pl.BlockSpec((tm, LANE), lambda i, j: (i, 0))],
        out_shape=[jax.ShapeDtypeStruct((seq, n_tiles * tn), BF16),
                   jax.ShapeDtypeStruct((seq, LANE), BF16)],
        scratch_shapes=[pltpu.VMEM((tm, d), BF16)],
        compiler_params=_cparams("parallel", "arbitrary"),
        name="in_proj",
    )(x2, g_pre, w_in_t, w_gate, b_gate, w_kr)


def _branch_a_kernel(gu_ref, gv_ref, sz_ref, lng_ref, lnb_ref, ws_ref, bs_ref, o_ref, *, tm):
    gv = gv_ref[...].astype(F32)
    mu = jnp.mean(gv, axis=-1, keepdims=True)
    cen = gv - mu
    var = jnp.mean(cen * cen, axis=-1, keepdims=True)
    vn = (cen * lax.rsqrt(var + EPS) * lng_ref[...] + lnb_ref[...]).astype(BF16)
    row = lax.broadcasted_iota(jnp.int32, (CHUNK, CHUNK), 0)
    col = lax.broadcasted_iota(jnp.int32, (CHUNK, CHUNK), 1)
    causal = col <= row
    for g in range(A_GROUPS):
        ws = jnp.where(causal, ws_ref[g], 0.0).astype(BF16)
        cs = slice(g * CHUNK, (g + 1) * CHUNK)
        for c in range(tm // CHUNK):
            rs = slice(c * CHUNK, (c + 1) * CHUNK)
            sv = _dot(ws, vn[rs, cs]) + bs_ref[g]
            y = gu_ref[rs, cs].astype(F32) * sv * sz_ref[rs, cs].astype(F32)
            o_ref[rs, cs] = y.astype(o_ref.dtype)


def _branch_a(proj, ln_g, ln_b, w_s, b_s_b, offs, tm):
    seq = proj.shape[0]
    w = A_GROUPS * CHUNK
    blk = lambda off: pl.BlockSpec((tm, w), lambda i, off=off: (i, off // w))
    const = lambda shape: pl.BlockSpec(shape, lambda i: (0,) * len(shape))
    return pl.pallas_call(
        functools.partial(_branch_a_kernel, tm=tm),
        grid=(seq // tm,),
        in_specs=[blk(offs["u"]), blk(offs["v"]), blk(offs["za"]),
                  const((1, w)), const((1, w)),
                  const((A_GROUPS, CHUNK, CHUNK)), const((A_GROUPS, CHUNK, CHUNK))],
        out_specs=pl.BlockSpec((tm, w), lambda i: (i, 0)),
        out_shape=jax.ShapeDtypeStruct((seq, w), BF16),
        compiler_params=_cparams("parallel"),
        name="branch_a",
    )(proj, proj, proj, ln_g, ln_b, w_s, b_s_b)


def _mla_prep_kernel(cq_ref, ckv_ref, kr_ref, tc_ref, ta_ref, tb_ref, qg_ref, kvg_ref,
                     wq_ref, wkv_ref, q_ref, k_ref, vt_ref):
    scale = QK_DIM ** -0.5 * LOG2E
    tab_c, tab_a, tab_b = tc_ref[...], ta_ref[...], tb_ref[...]
    cqn = _rms(cq_ref[...].astype(F32), qg_ref[...]).astype(BF16)
    ckvn = _rms(ckv_ref[...].astype(F32), kvg_ref[...]).astype(BF16)
    kpe = _rope(kr_ref[...].astype(F32), tab_c, tab_a, tab_b).astype(BF16)
    for h in range(MLA_HEADS):
        lo, mid, hi = h * HEAD_PAD, h * HEAD_PAD + LANE, (h + 1) * HEAD_PAD
        q = _dot(cqn, wq_ref[:, lo:hi]) * scale
        q_ref[:, lo:mid] = q[:, :LANE].astype(BF16)
        q_ref[:, mid:hi] = _rope(q[:, LANE:], tab_c, tab_a, tab_b).astype(BF16)
        kv = _dot(ckvn, wkv_ref[:, lo:hi])
        k_ref[:, lo:mid] = kv[:, :LANE].astype(BF16)
        k_ref[:, mid:hi] = kpe
        vt_ref[h * V_DIM:(h + 1) * V_DIM, :] = kv[:, LANE:].T.astype(BF16)


def _mla_prep(proj, kr, tabs, q_norm_g, kv_norm_g, wq_pad, wkv, offs, tm):
    seq = proj.shape[0]
    hw = MLA_HEADS * HEAD_PAD
    const = lambda shape: pl.BlockSpec(shape, lambda i: (0,) * len(shape))
    col = lambda off, w: pl.BlockSpec((tm, w), lambda i: (i, off // w))
    tab = pl.BlockSpec((tm, LANE), lambda i: (i, 0))
    return pl.pallas_call(
        _mla_prep_kernel,
        grid=(seq // tm,),
        in_specs=[col(offs["cq"], Q_LORA), col(offs["ckv"], KV_LORA),
                  tab, tab, tab, tab, const((1, Q_LORA)), const((1, KV_LORA)),
                  const((Q_LORA, hw)), const((KV_LORA, hw))],
        out_specs=[pl.BlockSpec((tm, hw), lambda i: (i, 0)),
                   pl.BlockSpec((tm, hw), lambda i: (i, 0)),
                   pl.BlockSpec((MLA_HEADS * V_DIM, tm), lambda i: (0, i))],
        out_shape=[jax.ShapeDtypeStruct((seq, hw), BF16),
                   jax.ShapeDtypeStruct((seq, hw), BF16),
                   jax.ShapeDtypeStruct((MLA_HEADS * V_DIM, seq), BF16)],
        compiler_params=_cparams("parallel"),
        name="mla_prep",
    )(proj, proj, kr, *tabs, q_norm_g, kv_norm_g, wq_pad, wkv)


def _flash_kernel(q_ref, k_ref, vt_ref, g_ref, o_ref, *scratch, tq, tk, cb):
    qi = pl.program_id(1)
    nq = q_ref.shape[0] // tq
    ncb = tq // cb
    base = pl.multiple_of(qi * tq, tq)
    take = lambda n: [scratch[i * ncb:(i + 1) * ncb] for i in range(n)]
    m_refs, l_refs, acc_refs, s0, s1, mn0, mn1, al0, al1 = take(9)
    s_refs, mnew_refs, alpha_refs = (s0, s1), (mn0, mn1), (al0, al1)
    for c in range(ncb):
        l_refs[c][...] = jnp.zeros(l_refs[c].shape, F32)
        acc_refs[c][...] = jnp.zeros(acc_refs[c].shape, F32)

    def scores(c, kstart, rows, slot, mask_off=None, qrow=None):
        qstart = pl.multiple_of((base if qrow is None else qrow) + c * cb, cb)
        s = _dot_nt(k_ref[pl.ds(kstart, rows), :], q_ref[pl.ds(qstart, cb), :])
        if mask_off is not None:
            key = lax.broadcasted_iota(jnp.int32, s.shape, 0)
            qry = lax.broadcasted_iota(jnp.int32, s.shape, 1)
            s = jnp.where(key <= qry + mask_off, s, -1e30)
        m_prev = m_refs[c][...] if qrow is None else jnp.full(m_refs[c].shape, -1e30, F32)
        m_new = jnp.maximum(m_prev, jnp.max(s, axis=0, keepdims=True))
        s_refs[slot][c][:rows, :] = s
        mnew_refs[slot][c][...] = m_new
        alpha_refs[slot][c][...] = jnp.exp2(m_prev - m_new)
        m_refs[c][...] = m_new

    def update(c, kstart, rows, slot):
        alpha = alpha_refs[slot][c][...]
        p = jnp.exp2(s_refs[slot][c][:rows, :] - mnew_refs[slot][c][...])
        l_refs[c][...] = alpha * l_refs[c][...] + jnp.sum(p, axis=0, keepdims=True)
        pv = _dot(vt_ref[:, pl.ds(kstart, rows)], p.astype(BF16))
        acc_refs[c][...] = alpha * acc_refs[c][...] + pv

    def kv(t):
        return pl.multiple_of(t * tk, tk)

    assert tq == 2 * tk

    @pl.when(qi > 0)
    def _():
        def pair(i):
            for c in range(ncb):
                scores(c, kv(2 * i + 1), tk, 1)
                update(c, kv(2 * i), tk, 0)
            for c in range(ncb):
                scores(c, kv(2 * i + 2), tk, 0)
                update(c, kv(2 * i + 1), tk, 1)

        def body(t, carry):
            for u in range(4):
                pair(4 * t + u)
            return carry

        n_pairs = qi - 1
        lax.fori_loop(0, lax.shift_right_logical(n_pairs, 2), body, 0)
        done = n_pairs & ~3

        @pl.when((n_pairs & 2) != 0)
        def _():
            pair(done)
            pair(done + 1)

        @pl.when((n_pairs & 1) != 0)
        def _():
            pair(done + (n_pairs & 2))

        for c in range(ncb):
            scores(c, kv(2 * qi - 1), tk, 1)
            update(c, kv(2 * qi - 2), tk, 0)

    @pl.when(qi == 0)
    def _():
        for c in range(ncb):
            m_refs[c][...] = jnp.full(m_refs[c].shape, -1e30, F32)
            s_refs[1][c][...] = jnp.zeros(s_refs[1][c].shape, F32)
            mnew_refs[1][c][...] = jnp.full(mnew_refs[1][c].shape, 1e30, F32)
            alpha_refs[1][c][...] = jnp.ones(alpha_refs[1][c].shape, F32)

    pending = kv(jnp.maximum(2 * qi - 1, 0))
    diag = [[], []]
    for d in range(2):
        for c in range(ncb):
            rows = min(tk, (c + 1) * cb - d * tk)
            if rows > 0:
                crosses = d * tk + rows - 1 > c * cb
                diag[d].append(((c, pl.multiple_of(base + d * tk, tk), rows, d),
                                c * cb - d * tk if crosses else None))
    assert [item[0] for item, _ in diag[0]] == list(range(ncb))
    for item, mask_off in diag[0]:
        scores(*item, mask_off)
        update(item[0], pending, tk, 1)
    for n, (item, mask_off) in enumerate(diag[1]):
        scores(*item, mask_off)
        update(*diag[0][n][0])
    rest = [item for item, _ in diag[0][len(diag[1]):] + diag[1]]
    assert len(rest) == ncb and all(rest[c - len(diag[1])][0::3] == (c, 0)
                                    for c in range(len(diag[1]), ncb))
    nxt = pl.multiple_of(jnp.minimum(qi + 1, nq - 1) * tq, tq)
    for c, item in enumerate(rest):
        scores(c, 0, tk, 0, qrow=nxt)
        update(*item)

    for c in range(ncb):
        o = (acc_refs[c][...] / l_refs[c][...]).T
        rs = slice(c * cb, (c + 1) * cb)
        o_ref[rs, :] = (o * g_ref[rs, :].astype(F32)).astype(o_ref.dtype)


def _flash(q, k, vt, proj, zb_off, tq, tk, cb):
    seq = q.shape[0]
    ncb = tq // cb
    stat = [pltpu.VMEM((1, cb), F32)] * ncb
    scratch = (stat + stat + [pltpu.VMEM((V_DIM, cb), F32)] * ncb
               + [pltpu.VMEM((tk, cb), F32)] * (2 * ncb)
               + stat * 4)
    return pl.pallas_call(
        functools.partial(_flash_kernel, tq=tq, tk=tk, cb=cb),
        grid=(MLA_HEADS, seq // tq),
        in_specs=[pl.BlockSpec((seq, HEAD_PAD), lambda h, i: (0, h)),
                  pl.BlockSpec((seq, HEAD_PAD), lambda h, i: (0, h)),
                  pl.BlockSpec((V_DIM, seq), lambda h, i: (h, 0)),
                  pl.BlockSpec((tq, V_DIM), lambda h, i: (i, zb_off // V_DIM + h))],
        out_specs=pl.BlockSpec((tq, V_DIM), lambda h, i: (i, h)),
        out_shape=jax.ShapeDtypeStruct((seq, MLA_HEADS * V_DIM), BF16),
        scratch_shapes=scratch,
        compiler_params=_cparams("arbitrary", "arbitrary"),
        name="flash_mla",
    )(q, k, vt, proj)


def _mem_kv_kernel(mem_ref, g_ref, w_ref, o_ref):
    mn = _rms(mem_ref[...], g_ref[...]).astype(BF16)
    o_ref[...] = _dot(mn, w_ref[...].astype(BF16)).astype(o_ref.dtype)


def _mem_kv(mem2, mem_norm_g, w_mem_kv, tn):
    m, d = mem2.shape
    n = w_mem_kv.shape[1]
    return pl.pallas_call(
        _mem_kv_kernel,
        grid=(n // tn,),
        in_specs=[pl.BlockSpec((m, d), lambda j: (0, 0)),
                  pl.BlockSpec((1, d), lambda j: (0, 0)),
                  pl.BlockSpec((d, tn), lambda j: (0, j))],
        out_specs=pl.BlockSpec((m, tn), lambda j: (0, j)),
        out_shape=jax.ShapeDtypeStruct((m, n), BF16),
        compiler_params=_cparams("parallel"),
        name="mem_kv",
    )(mem2, mem_norm_g, w_mem_kv)


def _mem_attn_kernel(q_ref, kv_ref, sz_ref, o_ref, *, width):
    hd = width // MEM_HEADS
    scale = hd ** -0.5
    for h in range(MEM_HEADS):
        cs = slice(h * hd, (h + 1) * hd)
        s = _dot_nt(q_ref[:, cs], kv_ref[:, cs]) * scale
        p = jnp.exp(s - jnp.max(s, axis=-1, keepdims=True))
        l = jnp.sum(p, axis=-1, keepdims=True)
        o = _dot(p.astype(BF16), kv_ref[:, width + h * hd:width + (h + 1) * hd]) / l
        o_ref[:, cs] = (o * sz_ref[:, cs].astype(F32)).astype(o_ref.dtype)


def _mem_attn(proj, kvm, offs, width, tm):
    seq = proj.shape[0]
    blk = lambda off: pl.BlockSpec((tm, width), lambda i: (i, off // width))
    return pl.pallas_call(
        functools.partial(_mem_attn_kernel, width=width),
        grid=(seq // tm,),
        in_specs=[blk(offs["qm"]), pl.BlockSpec(kvm.shape, lambda i: (0, 0)), blk(offs["zm"])],
        out_specs=pl.BlockSpec((tm, width), lambda i: (i, 0)),
        out_shape=jax.ShapeDtypeStruct((seq, width), BF16),
        compiler_params=_cparams("parallel"),
        name="mem_attn",
    )(proj, kvm, proj)


def _merge_kernel(ya_ref, yb_ref, ym_ref, w_ref, g0_ref, g1_ref, g2_ref, o_ref):
    acc = g0_ref[...].astype(F32) * _dot(ya_ref[...], w_ref[0])
    acc += g1_ref[...].astype(F32) * _dot(yb_ref[...], w_ref[1])
    acc += g2_ref[...].astype(F32) * _dot(ym_ref[...], w_ref[2])
    o_ref[...] = acc.astype(o_ref.dtype)


def _merge(y_a, y_b, y_m, w_branch, proj, gate_off, tm, tn):
    seq, width = y_a.shape
    d = w_branch.shape[2]
    yblk = pl.BlockSpec((tm, width), lambda i, j: (i, 0))
    gblk = lambda n: pl.BlockSpec((tm, tn), lambda i, j, n=n: (i, (gate_off + n * d) // tn + j))
    return pl.pallas_call(
        _merge_kernel,
        grid=(seq // tm, d // tn),
        in_specs=[yblk, yblk, yblk,
                  pl.BlockSpec((N_BRANCH, width, tn), lambda i, j: (0, 0, j)),
                  gblk(0), gblk(1), gblk(2)],
        out_specs=pl.BlockSpec((tm, tn), lambda i, j: (i, j)),
        out_shape=jax.ShapeDtypeStruct((seq, d), BF16),
        compiler_params=_cparams("parallel", "arbitrary"),
        name="merge",
    )(y_a, y_b, y_m, w_branch, proj, proj, proj)


def _out_kernel(m_ref, w_ref, x_ref, g_ref, o_ref):
    out = _dot(m_ref[...], w_ref[...])
    o_ref[...] = x_ref[...] + _rms(out, g_ref[...])


def _out_proj(merged, w_out, x2, g_post, tm):
    seq, d = x2.shape
    return pl.pallas_call(
        _out_kernel,
        grid=(seq // tm,),
        in_specs=[pl.BlockSpec((tm, d), lambda i: (i, 0)),
                  pl.BlockSpec((d, d), lambda i: (0, 0)),
                  pl.BlockSpec((tm, d), lambda i: (i, 0)),
                  pl.BlockSpec((1, d), lambda i: (0, 0))],
        out_specs=pl.BlockSpec((tm, d), lambda i: (i, 0)),
        out_shape=jax.ShapeDtypeStruct((seq, d), F32),
        compiler_params=_cparams("parallel"),
        name="out_proj",
    )(merged, w_out, x2, g_post)


def _layer(x2, mem2, tabs, g_pre, w_in, a_ln_g, a_ln_b, a_w_s, a_b_s, q_norm_g, w_uq,
           kv_norm_g, w_ukv, mem_norm_g, w_mem_kv, w_gate, b_gate, w_branch, w_out, g_post):
    seq, d = x2.shape
    tn = 1024
    w_in_t = jnp.swapaxes(w_in, 0, 1).astype(BF16)
    lat, tail = 3 * d, 3 * d + Q_LORA + KV_LORA
    w_kr = jnp.pad(w_in_t[tail:tail + QK_ROPE], ((0, LANE - QK_ROPE), (0, 0)))
    sections = (("u", _gelu, 0, d), ("v", _gelu, d, d), ("za", _silu, 2 * d, d),
                ("zb", _silu, tail + QK_ROPE, d), ("qm", _identity, tail + QK_ROPE + d, d),
                ("zm", _silu, tail + QK_ROPE + 2 * d, d), ("gate", _sigmoid, None, N_BRANCH * d),
                ("cq", _identity, lat, Q_LORA + KV_LORA))
    offs, tiles, col = {}, [], 0
    for name, act, row, width in sections:
        offs[name] = col
        tiles += [(act, None if row is None else row + t) for t in range(0, width, tn)]
        col += width
    offs["ckv"] = offs["cq"] + Q_LORA

    proj, kr = _projection(x2, g_pre.reshape(1, d), w_in_t, w_gate.astype(BF16),
                           b_gate.reshape(1, -1), w_kr, tuple(tiles), tm=1024, tn=tn)

    b_s_b = jnp.broadcast_to(a_b_s[:, :, None], (A_GROUPS, CHUNK, CHUNK))
    y_a = _branch_a(proj, a_ln_g.reshape(1, d), a_ln_b.reshape(1, d), a_w_s, b_s_b, offs, tm=512)

    wq_pad = jnp.pad(w_uq.reshape(Q_LORA, MLA_HEADS, QK_DIM),
                     ((0, 0), (0, 0), (0, HEAD_PAD - QK_DIM))).reshape(Q_LORA, -1).astype(BF16)
    q, k, vv = _mla_prep(proj, kr, tabs, q_norm_g.reshape(1, -1), kv_norm_g.reshape(1, -1),
                         wq_pad, w_ukv.astype(BF16), offs, tm=512)
    y_b = _flash(q, k, vv, proj, offs["zb"], tq=1024, tk=512, cb=256)

    kvm = _mem_kv(mem2, mem_norm_g.reshape(1, d), w_mem_kv, tn=512)
    y_m = _mem_attn(proj, kvm, offs, d, tm=1024)

    merged = _merge(y_a, y_b, y_m, w_branch.astype(BF16), proj, offs["gate"], tm=1024, tn=512)
    return _out_proj(merged, w_out.astype(BF16), x2, g_post.reshape(1, d), tm=512)


def kernel(x, mem, positions, g_pre, w_in, a_ln_g, a_ln_b, a_w_s, a_b_s, q_norm_g, w_uq,
           kv_norm_g, w_ukv, mem_norm_g, w_mem_kv, w_gate, b_gate, w_branch, w_out, g_post):
    batch, seq, d = x.shape
    depth = g_pre.shape[0]
    outs = []
    for b in range(batch):
        tabs = _rope_tables(positions[b], seq)
        xb = x[b]
        for l in range(depth):
            xb = _layer(xb, mem[b], tabs, g_pre[l], w_in[l], a_ln_g[l], a_ln_b[l], a_w_s[l],
                        a_b_s[l], q_norm_g[l], w_uq[l], kv_norm_g[l], w_ukv[l], mem_norm_g[l],
                        w_mem_kv[l], w_gate[l], b_gate[l], w_branch[l], w_out[l], g_post[l])
        outs.append(xb)
    return jnp.stack(outs, axis=0)
```

```python
import functools

import jax
import jax.numpy as jnp
from jax import lax
from jax.experimental import pallas as pl
from jax.experimental.pallas import tpu as pltpu

EPS = 1e-6
CHUNK = 128
A_GROUPS = 16
QK_NOPE = 128
QK_ROPE = 64
V_DIM = 128
MLA_HEADS = 16
Q_LORA = 512
KV_LORA = 512
QK_DIM = QK_NOPE + QK_ROPE
ROPE_THETA = 10000.0
LOG2E = 1.4426950408889634
GELU_C = 0.7978845608028654
PROJ_CHUNK = 256
MEM_HEADS = 4
N_BRANCH = 3
LANE = 128
HEAD_PAD = 256
VMEM_LIMIT = 52 * 1024 * 1024
MASKED = -1e30

TILES = {
    "in_proj": dict(tm=1024, tn=1024),
    "branch_a": dict(tm=512),
    "mla_prep": dict(tm=512),
    "flash": dict(tq=1024, tk=512, cb=PROJ_CHUNK),
    "mem_kv": dict(tn=512),
    "mem_attn": dict(tm=1024),
    "merge": dict(tm=1024, tn=512),
    "out_proj": dict(tm=512),
}

F32 = jnp.float32
BF16 = jnp.bfloat16


def _cparams(*sem):
    return pltpu.CompilerParams(dimension_semantics=sem, vmem_limit_bytes=VMEM_LIMIT)


def _dot(a, b):
    return jnp.dot(a, b, preferred_element_type=F32)


def _dot_nt(a, b):
    return lax.dot_general(a, b, (((1,), (1,)), ((), ())), preferred_element_type=F32)


def _gelu(x):
    half = 0.5 * x
    return half + half * jnp.tanh(x * (GELU_C + (GELU_C * 0.044715) * (x * x)))


def _sigmoid(x):
    return 0.5 + 0.5 * jnp.tanh(0.5 * x)


def _silu(x):
    half = 0.5 * x
    return half + half * jnp.tanh(half)


def _identity(x):
    return x


def _rms(xf, g):
    return xf * lax.rsqrt(jnp.mean(xf * xf, axis=-1, keepdims=True) + EPS) * g


def _rope_table_kernel(pos_ref, invf_ref, tc_ref, ta_ref, tb_ref):
    ang = pos_ref[...].astype(F32) * invf_ref[...]
    cos, sin = jnp.cos(ang), jnp.sin(ang)
    z32 = jnp.zeros_like(cos)
    z64 = jnp.concatenate([z32, z32], axis=0)
    tc_ref[...] = jnp.concatenate([cos, cos, z64], axis=0).T
    ta_ref[...] = jnp.concatenate([sin, z32, z64], axis=0).T
    tb_ref[...] = jnp.concatenate([z32, sin, z64], axis=0).T


def _rope_tables(positions, seq):
    half = QK_ROPE // 2
    inv_freq = 1.0 / (ROPE_THETA ** (jnp.arange(0, QK_ROPE, 2, dtype=F32) / QK_ROPE))
    return pl.pallas_call(
        _rope_table_kernel,
        out_shape=(jax.ShapeDtypeStruct((seq, LANE), F32),) * 3,
        compiler_params=pltpu.CompilerParams(vmem_limit_bytes=VMEM_LIMIT),
        name="rope_tables",
    )(positions.reshape(1, seq), inv_freq.reshape(half, 1))


def _rope(t, tab_c, tab_a, tab_b):
    down = pltpu.roll(t, 3 * LANE // 4, axis=1)
    up = pltpu.roll(t, LANE // 4, axis=1)
    return t * tab_c - down * tab_a + up * tab_b


def _proj_kernel(x_ref, g_ref, wt_ref, wg_ref, b_ref, wkr_ref, o_ref, kr_ref, h_ref, *, tiles):
    j = pl.program_id(1)

    def tile(act, is_gate):
        for lo in range(0, o_ref.shape[1], PROJ_CHUNK):
            cs = slice(lo, lo + PROJ_CHUNK)
            if is_gate:
                acc = _dot(h_ref[...], wg_ref[:, cs]) + b_ref[:, cs]
            else:
                acc = _dot_nt(h_ref[...], wt_ref[cs, :])
            o_ref[:, cs] = act(acc).astype(o_ref.dtype)

    kind = lambda t: (tiles[t][0], tiles[t][1] is None)

    @pl.when(j == 0)
    def _():
        h_ref[...] = _rms(x_ref[...], g_ref[...]).astype(BF16)
        tile(*kind(0))

    later = range(1, len(tiles))
    for k in sorted({kind(t) for t in later}, key=lambda k: (k[0].__name__, k[1])):
        @pl.when(functools.reduce(jnp.logical_or, [j == t for t in later if kind(t) == k]))
        def _(k=k):
            tile(*k)

    @pl.when(j == len(tiles) - 1)
    def _():
        kr_ref[...] = _dot_nt(h_ref[...], wkr_ref[...]).astype(kr_ref.dtype)


def _projection(x2, g_pre, w_in_t, w_gate, b_gate, w_kr, tiles, tm, tn):
    seq, d = x2.shape
    n_tiles = len(tiles)

    def pick(j, values):
        return functools.reduce(lambda acc, tv: jnp.where(j == tv[0], tv[1], acc),
                                enumerate(values), values[0])

    rows, gates, row, gate = [], [], 0, 0
    for act, src in tiles:
        if src is None:
            gates.append(gate)
            gate += 1
        else:
            row = src
            gates.append(max(gate - 1, 0))
        rows.append(row)
    return pl.pallas_call(
        functools.partial(_proj_kernel, tiles=tiles),
        grid=(seq // tm, n_tiles),
        in_specs=[
            pl.BlockSpec((tm, d), lambda i, j: (i, 0)),
            pl.BlockSpec((1, d), lambda i, j: (0, 0)),
            pl.BlockSpec((pl.Element(tn), pl.Element(d)),
                         lambda i, j: (pick(j, [r // QK_ROPE for r in rows]) * QK_ROPE, 0)),
            pl.BlockSpec((d, tn), lambda i, j: (0, pick(j, gates))),
            pl.BlockSpec((1, tn), lambda i, j: (0, pick(j, gates))),
            pl.BlockSpec((LANE, d), lambda i, j: (0, 0)),
        ],
        out_specs=[pl.BlockSpec((tm, tn), lambda i, j: (i, j)),
                   pl.BlockSpec((tm, LANE), lambda i, j: (i, 0))],
        out_shape=[jax.ShapeDtypeStruct((seq, n_tiles * tn), BF16),
                   jax.ShapeDtypeStruct((seq, LANE), BF16)],
        scratch_shapes=[pltpu.VMEM((tm, d), BF16)],
        compiler_params=_cparams("parallel", "arbitrary"),
        name="in_proj",
    )(x2, g_pre, w_in_t, w_gate, b_gate, w_kr)


def _branch_a_kernel(gu_ref, gv_ref, sz_ref, lng_ref, lnb_ref, ws_ref, bs_ref, o_ref, *, tm):
    gv = gv_ref[...].astype(F32)
    mu = jnp.mean(gv, axis=-1, keepdims=True)
    cen = gv - mu
    var = jnp.mean(cen * cen, axis=-1, keepdims=True)
    vn = (cen * lax.rsqrt(var + EPS) * lng_ref[...] + lnb_ref[...]).astype(BF16)
    row = lax.broadcasted_iota(jnp.int32, (CHUNK, CHUNK), 0)
    col = lax.broadcasted_iota(jnp.int32, (CHUNK, CHUNK), 1)
    causal = col <= row
    for g in range(A_GROUPS):
        ws = jnp.where(causal, ws_ref[g], 0.0).astype(BF16)
        cs = slice(g * CHUNK, (g + 1) * CHUNK)
        for c in range(tm // CHUNK):
            rs = slice(c * CHUNK, (c + 1) * CHUNK)
            sv = _dot(ws, vn[rs, cs]) + bs_ref[g]
            y = gu_ref[rs, cs].astype(F32) * sv * sz_ref[rs, cs].astype(F32)
            o_ref[rs, cs] = y.astype(o_ref.dtype)


def _branch_a(proj, ln_g, ln_b, w_s, b_s_b, offs, tm):
    seq = proj.shape[0]
    w = A_GROUPS * CHUNK
    blk = lambda off: pl.BlockSpec((tm, w), lambda i, off=off: (i, off // w))
    const = lambda shape: pl.BlockSpec(shape, lambda i: (0,) * len(shape))
    return pl.pallas_call(
        functools.partial(_branch_a_kernel, tm=tm),
        grid=(seq // tm,),
        in_specs=[blk(offs["u"]), blk(offs["v"]), blk(offs["za"]),
                  const((1, w)), const((1, w)),
                  const((A_GROUPS, CHUNK, CHUNK)), const((A_GROUPS, CHUNK, CHUNK))],
        out_specs=pl.BlockSpec((tm, w), lambda i: (i, 0)),
        out_shape=jax.ShapeDtypeStruct((seq, w), BF16),
        compiler_params=_cparams("parallel"),
        name="branch_a",
    )(proj, proj, proj, ln_g, ln_b, w_s, b_s_b)


def _mla_prep_kernel(cq_ref, ckv_ref, kr_ref, tc_ref, ta_ref, tb_ref, qg_ref, kvg_ref,
                     wq_ref, wkv_ref, q_ref, k_ref, vt_ref):
    scale = QK_DIM ** -0.5 * LOG2E
    tab_c, tab_a, tab_b = tc_ref[...], ta_ref[...], tb_ref[...]
    cqn = _rms(cq_ref[...].astype(F32), qg_ref[...]).astype(BF16)
    ckvn = _rms(ckv_ref[...].astype(F32), kvg_ref[...]).astype(BF16)
    kpe = _rope(kr_ref[...].astype(F32), tab_c, tab_a, tab_b).astype(BF16)
    for h in range(MLA_HEADS):
        lo, mid, hi = h * HEAD_PAD, h * HEAD_PAD + LANE, (h + 1) * HEAD_PAD
        q = _dot(cqn, wq_ref[:, lo:hi]) * scale
        q_ref[:, lo:mid] = q[:, :LANE].astype(BF16)
        q_ref[:, mid:hi] = _rope(q[:, LANE:], tab_c, tab_a, tab_b).astype(BF16)
        kv = _dot(ckvn, wkv_ref[:, lo:hi])
        k_ref[:, lo:mid] = kv[:, :LANE].astype(BF16)
        k_ref[:, mid:hi] = kpe
        vt_ref[h * V_DIM:(h + 1) * V_DIM, :] = kv[:, LANE:].T.astype(BF16)


def _mla_prep(proj, kr, tabs, q_norm_g, kv_norm_g, wq_pad, wkv, offs, tm):
    seq = proj.shape[0]
    hw = MLA_HEADS * HEAD_PAD
    const = lambda shape: pl.BlockSpec(shape, lambda i: (0,) * len(shape))
    col = lambda off, w: pl.BlockSpec((tm, w), lambda i: (i, off // w))
    tab = pl.BlockSpec((tm, LANE), lambda i: (i, 0))
    return pl.pallas_call(
        _mla_prep_kernel,
        grid=(seq // tm,),
        in_specs=[col(offs["cq"], Q_LORA), col(offs["ckv"], KV_LORA),
                  tab, tab, tab, tab, const((1, Q_LORA)), const((1, KV_LORA)),
                  const((Q_LORA, hw)), const((KV_LORA, hw))],
        out_specs=[pl.BlockSpec((tm, hw), lambda i: (i, 0)),
                   pl.BlockSpec((tm, hw), lambda i: (i, 0)),
                   pl.BlockSpec((MLA_HEADS * V_DIM, tm), lambda i: (0, i))],
        out_shape=[jax.ShapeDtypeStruct((seq, hw), BF16),
                   jax.ShapeDtypeStruct((seq, hw), BF16),
                   jax.ShapeDtypeStruct((MLA_HEADS * V_DIM, seq), BF16)],
        compiler_params=_cparams("parallel"),
        name="mla_prep",
    )(proj, proj, kr, *tabs, q_norm_g, kv_norm_g, wq_pad, wkv)


def _flash_kernel(q_ref, k_ref, vt_ref, g_ref, o_ref, *scratch, tq, tk, cb):
    qi = pl.program_id(1)
    nq = q_ref.shape[0] // tq
    ncb = tq // cb
    base = pl.multiple_of(qi * tq, tq)
    take = lambda n: [scratch[i * ncb:(i + 1) * ncb] for i in range(n)]
    m_refs, l_refs, acc_refs, s0, s1, mn0, mn1, al0, al1 = take(9)
    s_refs, mnew_refs, alpha_refs = (s0, s1), (mn0, mn1), (al0, al1)
    for c in range(ncb):
        l_refs[c][...] = jnp.zeros(l_refs[c].shape, F32)
        acc_refs[c][...] = jnp.zeros(acc_refs[c].shape, F32)

    def scores(c, kstart, rows, slot, mask_off=None, qrow=None):
        qstart = pl.multiple_of((base if qrow is None else qrow) + c * cb, cb)
        s = _dot_nt(k_ref[pl.ds(kstart, rows), :], q_ref[pl.ds(qstart, cb), :])
        if mask_off is not None:
            key = lax.broadcasted_iota(jnp.int32, s.shape, 0)
            qry = lax.broadcasted_iota(jnp.int32, s.shape, 1)
            s = jnp.where(key <= qry + mask_off, s, MASKED)
        m_prev = m_refs[c][...] if qrow is None else jnp.full(m_refs[c].shape, MASKED, F32)
        m_new = jnp.maximum(m_prev, jnp.max(s, axis=0, keepdims=True))
        s_refs[slot][c][:rows, :] = s
        mnew_refs[slot][c][...] = m_new
        alpha_refs[slot][c][...] = jnp.exp2(m_prev - m_new)
        m_refs[c][...] = m_new

    def update(c, kstart, rows, slot):
        alpha = alpha_refs[slot][c][...]
        p = jnp.exp2(s_refs[slot][c][:rows, :] - mnew_refs[slot][c][...])
        l_refs[c][...] = alpha * l_refs[c][...] + jnp.sum(p, axis=0, keepdims=True)
        pv = _dot(vt_ref[:, pl.ds(kstart, rows)], p.astype(BF16))
        acc_refs[c][...] = alpha * acc_refs[c][...] + pv

    def kv(t):
        return pl.multiple_of(t * tk, tk)

    assert tq == 2 * tk

    @pl.when(qi > 0)
    def _():
        def pair(i):
            for c in range(ncb):
                scores(c, kv(2 * i + 1), tk, 1)
                update(c, kv(2 * i), tk, 0)
            for c in range(ncb):
                scores(c, kv(2 * i + 2), tk, 0)
                update(c, kv(2 * i + 1), tk, 1)

        def body(t, carry):
            for u in range(4):
                pair(4 * t + u)
            return carry

        n_pairs = qi - 1
        lax.fori_loop(0, lax.shift_right_logical(n_pairs, 2), body, 0)
        done = n_pairs & ~3

        @pl.when((n_pairs & 2) != 0)
        def _():
            pair(done)
            pair(done + 1)

        @pl.when((n_pairs & 1) != 0)
        def _():
            pair(done + (n_pairs & 2))

        for c in range(ncb):
            scores(c, kv(2 * qi - 1), tk, 1)
            update(c, kv(2 * qi - 2), tk, 0)

    @pl.when(qi == 0)
    def _():
        for c in range(ncb):
            m_refs[c][...] = jnp.full(m_refs[c].shape, MASKED, F32)
            s_refs[1][c][...] = jnp.zeros(s_refs[1][c].shape, F32)
            mnew_refs[1][c][...] = jnp.full(mnew_refs[1][c].shape, -MASKED, F32)
            alpha_refs[1][c][...] = jnp.ones(alpha_refs[1][c].shape, F32)

    pending = kv(jnp.maximum(2 * qi - 1, 0))
    diag = [[], []]
    for d in range(2):
        for c in range(ncb):
            rows = min(tk, (c + 1) * cb - d * tk)
            if rows > 0:
                crosses = d * tk + rows - 1 > c * cb
                diag[d].append(((c, pl.multiple_of(base + d * tk, tk), rows, d),
                                c * cb - d * tk if crosses else None))
    assert [item[0] for item, _ in diag[0]] == list(range(ncb))
    for item, mask_off in diag[0]:
        scores(*item, mask_off)
        update(item[0], pending, tk, 1)
    for n, (item, mask_off) in enumerate(diag[1]):
        scores(*item, mask_off)
        update(*diag[0][n][0])
    rest = [item for item, _ in diag[0][len(diag[1]):] + diag[1]]
    assert len(rest) == ncb and all(rest[c - len(diag[1])][0::3] == (c, 0)
                                    for c in range(len(diag[1]), ncb))
    nxt = pl.multiple_of(jnp.minimum(qi + 1, nq - 1) * tq, tq)
    for c, item in enumerate(rest):
        scores(c, 0, tk, 0, qrow=nxt)
        update(*item)

    for c in range(ncb):
        o = (acc_refs[c][...] / l_refs[c][...]).T
        rs = slice(c * cb, (c + 1) * cb)
        o_ref[rs, :] = (o * g_ref[rs, :].astype(F32)).astype(o_ref.dtype)


def _flash(q, k, vt, proj, zb_off, tq, tk, cb):
    seq = q.shape[0]
    ncb = tq // cb
    stat = [pltpu.VMEM((1, cb), F32)] * ncb
    scratch = (stat + stat + [pltpu.VMEM((V_DIM, cb), F32)] * ncb
               + [pltpu.VMEM((tk, cb), F32)] * (2 * ncb)
               + stat * 4)
    return pl.pallas_call(
        functools.partial(_flash_kernel, tq=tq, tk=tk, cb=cb),
        grid=(MLA_HEADS, seq // tq),
        in_specs=[pl.BlockSpec((seq, HEAD_PAD), lambda h, i: (0, h)),
                  pl.BlockSpec((seq, HEAD_PAD), lambda h, i: (0, h)),
                  pl.BlockSpec((V_DIM, seq), lambda h, i: (h, 0)),
                  pl.BlockSpec((tq, V_DIM), lambda h, i: (i, zb_off // V_DIM + h))],
        out_specs=pl.BlockSpec((tq, V_DIM), lambda h, i: (i, h)),
        out_shape=jax.ShapeDtypeStruct((seq, MLA_HEADS * V_DIM), BF16),
        scratch_shapes=scratch,
        compiler_params=_cparams("arbitrary", "arbitrary"),
        name="flash_mla",
    )(q, k, vt, proj)


def _mem_kv_kernel(mem_ref, g_ref, w_ref, o_ref):
    mn = _rms(mem_ref[...], g_ref[...]).astype(BF16)
    o_ref[...] = _dot(mn, w_ref[...].astype(BF16)).astype(o_ref.dtype)


def _mem_kv(mem2, mem_norm_g, w_mem_kv, tn):
    m, d = mem2.shape
    n = w_mem_kv.shape[1]
    return pl.pallas_call(
        _mem_kv_kernel,
        grid=(n // tn,),
        in_specs=[pl.BlockSpec((m, d), lambda j: (0, 0)),
                  pl.BlockSpec((1, d), lambda j: (0, 0)),
                  pl.BlockSpec((d, tn), lambda j: (0, j))],
        out_specs=pl.BlockSpec((m, tn), lambda j: (0, j)),
        out_shape=jax.ShapeDtypeStruct((m, n), BF16),
        compiler_params=_cparams("parallel"),
        name="mem_kv",
    )(mem2, mem_norm_g, w_mem_kv)


def _mem_attn_kernel(q_ref, kv_ref, sz_ref, o_ref, *, width):
    hd = width // MEM_HEADS
    scale = hd ** -0.5
    for h in range(MEM_HEADS):
        cs = slice(h * hd, (h + 1) * hd)
        s = _dot_nt(q_ref[:, cs], kv_ref[:, cs]) * scale
        p = jnp.exp(s - jnp.max(s, axis=-1, keepdims=True))
        l = jnp.sum(p, axis=-1, keepdims=True)
        o = _dot(p.astype(BF16), kv_ref[:, width + h * hd:width + (h + 1) * hd]) / l
        o_ref[:, cs] = (o * sz_ref[:, cs].astype(F32)).astype(o_ref.dtype)


def _mem_attn(proj, kvm, offs, width, tm):
    seq = proj.shape[0]
    blk = lambda off: pl.BlockSpec((tm, width), lambda i: (i, off // width))
    return pl.pallas_call(
        functools.partial(_mem_attn_kernel, width=width),
        grid=(seq // tm,),
        in_specs=[blk(offs["qm"]), pl.BlockSpec(kvm.shape, lambda i: (0, 0)), blk(offs["zm"])],
        out_specs=pl.BlockSpec((tm, width), lambda i: (i, 0)),
        out_shape=jax.ShapeDtypeStruct((seq, width), BF16),
        compiler_params=_cparams("parallel"),
        name="mem_attn",
    )(proj, kvm, proj)


def _merge_kernel(ya_ref, yb_ref, ym_ref, w_ref, g0_ref, g1_ref, g2_ref, o_ref):
    acc = g0_ref[...].astype(F32) * _dot(ya_ref[...], w_ref[0])
    acc += g1_ref[...].astype(F32) * _dot(yb_ref[...], w_ref[1])
    acc += g2_ref[...].astype(F32) * _dot(ym_ref[...], w_ref[2])
    o_ref[...] = acc.astype(o_ref.dtype)


def _merge(y_a, y_b, y_m, w_branch, proj, gate_off, tm, tn):
    seq, width = y_a.shape
    d = w_branch.shape[2]
    yblk = pl.BlockSpec((tm, width), lambda i, j: (i, 0))
    gblk = lambda n: pl.BlockSpec((tm, tn), lambda i, j, n=n: (i, (gate_off + n * d) // tn + j))
    return pl.pallas_call(
        _merge_kernel,
        grid=(seq // tm, d // tn),
        in_specs=[yblk, yblk, yblk,
                  pl.BlockSpec((N_BRANCH, width, tn), lambda i, j: (0, 0, j)),
                  gblk(0), gblk(1), gblk(2)],
        out_specs=pl.BlockSpec((tm, tn), lambda i, j: (i, j)),
        out_shape=jax.ShapeDtypeStruct((seq, d), BF16),
        compiler_params=_cparams("parallel", "arbitrary"),
        name="merge",
    )(y_a, y_b, y_m, w_branch, proj, proj, proj)


def _out_kernel(m_ref, w_ref, x_ref, g_ref, o_ref):
    out = _dot(m_ref[...], w_ref[...])
    o_ref[...] = x_ref[...] + _rms(out, g_ref[...])


def _out_proj(merged, w_out, x2, g_post, tm):
    seq, d = x2.shape
    return pl.pallas_call(
        _out_kernel,
        grid=(seq // tm,),
        in_specs=[pl.BlockSpec((tm, d), lambda i: (i, 0)),
                  pl.BlockSpec((d, d), lambda i: (0, 0)),
                  pl.BlockSpec((tm, d), lambda i: (i, 0)),
                  pl.BlockSpec((1, d), lambda i: (0, 0))],
        out_specs=pl.BlockSpec((tm, d), lambda i: (i, 0)),
        out_shape=jax.ShapeDtypeStruct((seq, d), F32),
        compiler_params=_cparams("parallel"),
        name="out_proj",
    )(merged, w_out, x2, g_post)


def _layer(x2, mem2, tabs, g_pre, w_in, a_ln_g, a_ln_b, a_w_s, a_b_s, q_norm_g, w_uq,
           kv_norm_g, w_ukv, mem_norm_g, w_mem_kv, w_gate, b_gate, w_branch, w_out, g_post):
    seq, d = x2.shape
    tn = TILES["in_proj"]["tn"]
    w_in_t = jnp.swapaxes(w_in, 0, 1).astype(BF16)
    lat, tail = 3 * d, 3 * d + Q_LORA + KV_LORA
    w_kr = jnp.pad(w_in_t[tail:tail + QK_ROPE], ((0, LANE - QK_ROPE), (0, 0)))
    sections = (("u", _gelu, 0, d), ("v", _gelu, d, d), ("za", _silu, 2 * d, d),
                ("zb", _silu, tail + QK_ROPE, d), ("qm", _identity, tail + QK_ROPE + d, d),
                ("zm", _silu, tail + QK_ROPE + 2 * d, d), ("gate", _sigmoid, None, N_BRANCH * d),
                ("cq", _identity, lat, Q_LORA + KV_LORA))
    offs, tiles, col = {}, [], 0
    for name, act, row, width in sections:
        offs[name] = col
        tiles += [(act, None if row is None else row + t) for t in range(0, width, tn)]
        col += width
    offs["ckv"] = offs["cq"] + Q_LORA

    proj, kr = _projection(x2, g_pre.reshape(1, d), w_in_t, w_gate.astype(BF16),
                           b_gate.reshape(1, -1), w_kr, tuple(tiles), **TILES["in_proj"])

    b_s_b = jnp.broadcast_to(a_b_s[:, :, None], (A_GROUPS, CHUNK, CHUNK))
    y_a = _branch_a(proj, a_ln_g.reshape(1, d), a_ln_b.reshape(1, d), a_w_s, b_s_b, offs,
                    **TILES["branch_a"])

    wq_pad = jnp.pad(w_uq.reshape(Q_LORA, MLA_HEADS, QK_DIM),
                     ((0, 0), (0, 0), (0, HEAD_PAD - QK_DIM))).reshape(Q_LORA, -1).astype(BF16)
    q, k, vv = _mla_prep(proj, kr, tabs, q_norm_g.reshape(1, -1), kv_norm_g.reshape(1, -1),
                         wq_pad, w_ukv.astype(BF16), offs, **TILES["mla_prep"])
    y_b = _flash(q, k, vv, proj, offs["zb"], **TILES["flash"])

    kvm = _mem_kv(mem2, mem_norm_g.reshape(1, d), w_mem_kv, **TILES["mem_kv"])
    y_m = _mem_attn(proj, kvm, offs, d, **TILES["mem_attn"])

    merged = _merge(y_a, y_b, y_m, w_branch.astype(BF16), proj, offs["gate"], **TILES["merge"])
    return _out_proj(merged, w_out.astype(BF16), x2, g_post.reshape(1, d), **TILES["out_proj"])


def kernel(x, mem, positions, g_pre, w_in, a_ln_g, a_ln_b, a_w_s, a_b_s, q_norm_g, w_uq,
           kv_norm_g, w_ukv, mem_norm_g, w_mem_kv, w_gate, b_gate, w_branch, w_out, g_post):
    batch, seq, d = x.shape
    depth = g_pre.shape[0]
    assert d == A_GROUPS * CHUNK == MLA_HEADS * V_DIM
    assert w_in.shape[-1] == 6 * d + Q_LORA + KV_LORA + QK_ROPE
    assert all(seq % t.get("tm", t.get("tq", 1)) == 0 for t in TILES.values())
    outs = []
    for b in range(batch):
        tabs = _rope_tables(positions[b], seq)
        xb = x[b]
        for l in range(depth):
            xb = _layer(xb, mem[b], tabs, g_pre[l], w_in[l], a_ln_g[l], a_ln_b[l], a_w_s[l],
                        a_b_s[l], q_norm_g[l], w_uq[l], kv_norm_g[l], w_ukv[l], mem_norm_g[l],
                        w_mem_kv[l], w_gate[l], b_gate[l], w_branch[l], w_out[l], g_post[l])
        outs.append(xb)
    return jnp.stack(outs, axis=0)
```

```python
import functools

import jax
import jax.numpy as jnp
from jax import lax
from jax.experimental import pallas as pl
from jax.experimental.pallas import tpu as pltpu

EPS = 1e-6
CHUNK = 128
A_GROUPS = 16
QK_NOPE = 128
QK_ROPE = 64
V_DIM = 128
MLA_HEADS = 16
Q_LORA = 512
KV_LORA = 512
QK_DIM = QK_NOPE + QK_ROPE
ROPE_THETA = 10000.0
LOG2E = 1.4426950408889634
GELU_C = 0.7978845608028654
PROJ_CHUNK = 256
MEM_HEADS = 4
N_BRANCH = 3
LANE = 128
HEAD_PAD = 256
VMEM_LIMIT = 52 * 1024 * 1024
PROJ_VMEM_LIMIT = 56 * 1024 * 1024
MASKED = -1e30

TILES = {
    "in_proj": dict(tm=1024, tn=1024),
    "branch_a": dict(tm=512),
    "mla_prep": dict(tm=512),
    "flash": dict(tq=1024, tk=512, cb=PROJ_CHUNK),
    "mem_kv": dict(tn=512),
    "mem_attn": dict(tm=1024),
    "merge": dict(tm=1024, tn=512),
    "out_proj": dict(tm=512),
}

F32 = jnp.float32
BF16 = jnp.bfloat16


def _cparams(*sem):
    return pltpu.CompilerParams(dimension_semantics=sem, vmem_limit_bytes=VMEM_LIMIT)


def _dot(a, b):
    return jnp.dot(a, b, preferred_element_type=F32)


def _dot_nt(a, b):
    return lax.dot_general(a, b, (((1,), (1,)), ((), ())), preferred_element_type=F32)


def _gelu(x):
    half = 0.5 * x
    return half + half * jnp.tanh(x * (GELU_C + (GELU_C * 0.044715) * (x * x)))


def _sigmoid(x):
    return 0.5 + 0.5 * jnp.tanh(0.5 * x)


def _silu(x):
    half = 0.5 * x
    return half + half * jnp.tanh(half)


def _identity(x):
    return x


def _rms(xf, g):
    return xf * lax.rsqrt(jnp.mean(xf * xf, axis=-1, keepdims=True) + EPS) * g


def _rope_table_kernel(pos_ref, invf_ref, tc_ref, ta_ref, tb_ref):
    ang = pos_ref[...].astype(F32) * invf_ref[...]
    cos, sin = jnp.cos(ang), jnp.sin(ang)
    z32 = jnp.zeros_like(cos)
    z64 = jnp.concatenate([z32, z32], axis=0)
    tc_ref[...] = jnp.concatenate([cos, cos, z64], axis=0).T
    ta_ref[...] = jnp.concatenate([sin, z32, z64], axis=0).T
    tb_ref[...] = jnp.concatenate([z32, sin, z64], axis=0).T


def _rope_tables(positions, seq):
    half = QK_ROPE // 2
    inv_freq = 1.0 / (ROPE_THETA ** (jnp.arange(0, QK_ROPE, 2, dtype=F32) / QK_ROPE))
    return pl.pallas_call(
        _rope_table_kernel,
        out_shape=(jax.ShapeDtypeStruct((seq, LANE), F32),) * 3,
        compiler_params=pltpu.CompilerParams(vmem_limit_bytes=VMEM_LIMIT),
        name="rope_tables",
    )(positions.reshape(1, seq), inv_freq.reshape(half, 1))


def _rope(t, tab_c, tab_a, tab_b):
    down = pltpu.roll(t, 3 * LANE // 4, axis=1)
    up = pltpu.roll(t, LANE // 4, axis=1)
    return t * tab_c - down * tab_a + up * tab_b


def _proj_kernel(x_ref, g_ref, wt_ref, wg_ref, b_ref, wkr_ref, o_ref, kr_ref, h_ref, *, tiles):
    j = pl.program_id(1)

    def tile(act, is_gate):
        for lo in range(0, o_ref.shape[1], PROJ_CHUNK):
            cs = slice(lo, lo + PROJ_CHUNK)
            if is_gate:
                acc = _dot(h_ref[...], wg_ref[:, cs]) + b_ref[:, cs]
            else:
                acc = _dot_nt(h_ref[...], wt_ref[cs, :].astype(BF16))
            o_ref[:, cs] = act(acc).astype(o_ref.dtype)

    kind = lambda t: (tiles[t][0], tiles[t][1] is None)

    @pl.when(j == 0)
    def _():
        h_ref[...] = _rms(x_ref[...], g_ref[...]).astype(BF16)
        tile(*kind(0))

    later = range(1, len(tiles))
    for k in sorted({kind(t) for t in later}, key=lambda k: (k[0].__name__, k[1])):
        @pl.when(functools.reduce(jnp.logical_or, [j == t for t in later if kind(t) == k]))
        def _(k=k):
            tile(*k)

    @pl.when(j == len(tiles) - 1)
    def _():
        kr_ref[...] = _dot_nt(h_ref[...], wkr_ref[...]).astype(kr_ref.dtype)


def _projection(x2, g_pre, w_in_t, w_gate, b_gate, w_kr, tiles, tm, tn):
    seq, d = x2.shape
    n_tiles = len(tiles)

    def pick(j, values):
        return functools.reduce(lambda acc, tv: jnp.where(j == tv[0], tv[1], acc),
                                enumerate(values), values[0])

    rows, gates, row, gate = [], [], 0, 0
    for act, src in tiles:
        if src is None:
            gates.append(gate)
            gate += 1
        else:
            row = src
            gates.append(max(gate - 1, 0))
        rows.append(row)
    return pl.pallas_call(
        functools.partial(_proj_kernel, tiles=tiles),
        grid=(seq // tm, n_tiles),
        in_specs=[
            pl.BlockSpec((tm, d), lambda i, j: (i, 0)),
            pl.BlockSpec((1, d), lambda i, j: (0, 0)),
            pl.BlockSpec((pl.Element(tn), pl.Element(d)),
                         lambda i, j: (pick(j, [r // QK_ROPE for r in rows]) * QK_ROPE, 0)),
            pl.BlockSpec((d, tn), lambda i, j: (0, pick(j, gates))),
            pl.BlockSpec((1, tn), lambda i, j: (0, pick(j, gates))),
            pl.BlockSpec((LANE, d), lambda i, j: (0, 0)),
        ],
        out_specs=[pl.BlockSpec((tm, tn), lambda i, j: (i, j)),
                   pl.BlockSpec((tm, LANE), lambda i, j: (i, 0))],
        out_shape=[jax.ShapeDtypeStruct((seq, n_tiles * tn), BF16),
                   jax.ShapeDtypeStruct((seq, LANE), BF16)],
        scratch_shapes=[pltpu.VMEM((tm, d), BF16)],
        compiler_params=pltpu.CompilerParams(dimension_semantics=("parallel", "arbitrary"),
                                             vmem_limit_bytes=PROJ_VMEM_LIMIT),
        name="in_proj",
    )(x2, g_pre, w_in_t, w_gate, b_gate, w_kr)


def _branch_a_kernel(gu_ref, gv_ref, sz_ref, lng_ref, lnb_ref, ws_ref, bs_ref, o_ref, *, tm):
    gv = gv_ref[...].astype(F32)
    mu = jnp.mean(gv, axis=-1, keepdims=True)
    cen = gv - mu
    var = jnp.mean(cen * cen, axis=-1, keepdims=True)
    vn = (cen * lax.rsqrt(var + EPS) * lng_ref[...] + lnb_ref[...]).astype(BF16)
    row = lax.broadcasted_iota(jnp.int32, (CHUNK, CHUNK), 0)
    col = lax.broadcasted_iota(jnp.int32, (CHUNK, CHUNK), 1)
    causal = col <= row
    for g in range(A_GROUPS):
        ws = jnp.where(causal, ws_ref[g], 0.0).astype(BF16)
        cs = slice(g * CHUNK, (g + 1) * CHUNK)
        for c in range(tm // CHUNK):
            rs = slice(c * CHUNK, (c + 1) * CHUNK)
            sv = _dot(ws, vn[rs, cs]) + bs_ref[g]
            y = gu_ref[rs, cs].astype(F32) * sv * sz_ref[rs, cs].astype(F32)
            o_ref[rs, cs] = y.astype(o_ref.dtype)


def _branch_a(proj, ln_g, ln_b, w_s, b_s_b, offs, tm):
    seq = proj.shape[0]
    w = A_GROUPS * CHUNK
    blk = lambda off: pl.BlockSpec((tm, w), lambda i, off=off: (i, off // w))
    const = lambda shape: pl.BlockSpec(shape, lambda i: (0,) * len(shape))
    return pl.pallas_call(
        functools.partial(_branch_a_kernel, tm=tm),
        grid=(seq // tm,),
        in_specs=[blk(offs["u"]), blk(offs["v"]), blk(offs["za"]),
                  const((1, w)), const((1, w)),
                  const((A_GROUPS, CHUNK, CHUNK)), const((A_GROUPS, CHUNK, CHUNK))],
        out_specs=pl.BlockSpec((tm, w), lambda i: (i, 0)),
        out_shape=jax.ShapeDtypeStruct((seq, w), BF16),
        compiler_params=_cparams("parallel"),
        name="branch_a",
    )(proj, proj, proj, ln_g, ln_b, w_s, b_s_b)


def _mla_prep_kernel(cq_ref, ckv_ref, kr_ref, tc_ref, ta_ref, tb_ref, qg_ref, kvg_ref,
                     wq_ref, wkv_ref, q_ref, k_ref, vt_ref):
    scale = QK_DIM ** -0.5 * LOG2E
    tab_c, tab_a, tab_b = tc_ref[...], ta_ref[...], tb_ref[...]
    cqn = _rms(cq_ref[...].astype(F32), qg_ref[...]).astype(BF16)
    ckvn = _rms(ckv_ref[...].astype(F32), kvg_ref[...]).astype(BF16)
    kpe = _rope(kr_ref[...].astype(F32), tab_c, tab_a, tab_b).astype(BF16)
    for h in range(MLA_HEADS):
        lo, mid, hi = h * HEAD_PAD, h * HEAD_PAD + LANE, (h + 1) * HEAD_PAD
        q = _dot(cqn, wq_ref[:, lo:hi]) * scale
        q_ref[:, lo:mid] = q[:, :LANE].astype(BF16)
        q_ref[:, mid:hi] = _rope(q[:, LANE:], tab_c, tab_a, tab_b).astype(BF16)
        kv = _dot(ckvn, wkv_ref[:, lo:hi])
        k_ref[:, lo:mid] = kv[:, :LANE].astype(BF16)
        k_ref[:, mid:hi] = kpe
        vt_ref[h * V_DIM:(h + 1) * V_DIM, :] = kv[:, LANE:].T.astype(BF16)


def _mla_prep(proj, kr, tabs, q_norm_g, kv_norm_g, wq_pad, wkv, offs, tm):
    seq = proj.shape[0]
    hw = MLA_HEADS * HEAD_PAD
    const = lambda shape: pl.BlockSpec(shape, lambda i: (0,) * len(shape))
    col = lambda off, w: pl.BlockSpec((tm, w), lambda i: (i, off // w))
    tab = pl.BlockSpec((tm, LANE), lambda i: (i, 0))
    return pl.pallas_call(
        _mla_prep_kernel,
        grid=(seq // tm,),
        in_specs=[col(offs["cq"], Q_LORA), col(offs["ckv"], KV_LORA),
                  tab, tab, tab, tab, const((1, Q_LORA)), const((1, KV_LORA)),
                  const((Q_LORA, hw)), const((KV_LORA, hw))],
        out_specs=[pl.BlockSpec((tm, hw), lambda i: (i, 0)),
                   pl.BlockSpec((tm, hw), lambda i: (i, 0)),
                   pl.BlockSpec((MLA_HEADS * V_DIM, tm), lambda i: (0, i))],
        out_shape=[jax.ShapeDtypeStruct((seq, hw), BF16),
                   jax.ShapeDtypeStruct((seq, hw), BF16),
                   jax.ShapeDtypeStruct((MLA_HEADS * V_DIM, seq), BF16)],
        compiler_params=_cparams("parallel"),
        name="mla_prep",
    )(proj, proj, kr, *tabs, q_norm_g, kv_norm_g, wq_pad, wkv)


def _flash_kernel(q_ref, k_ref, vt_ref, g_ref, o_ref, *scratch, tq, tk, cb):
    qi = pl.program_id(1)
    nq = q_ref.shape[0] // tq
    ncb = tq // cb
    base = pl.multiple_of(qi * tq, tq)
    take = lambda n: [scratch[i * ncb:(i + 1) * ncb] for i in range(n)]
    m_refs, l_refs, acc_refs, s0, s1, mn0, mn1, al0, al1 = take(9)
    s_refs, mnew_refs, alpha_refs = (s0, s1), (mn0, mn1), (al0, al1)
    for c in range(ncb):
        l_refs[c][...] = jnp.zeros(l_refs[c].shape, F32)
        acc_refs[c][...] = jnp.zeros(acc_refs[c].shape, F32)

    def scores(c, kstart, rows, slot, mask_off=None, qrow=None):
        qstart = pl.multiple_of((base if qrow is None else qrow) + c * cb, cb)
        s = _dot_nt(k_ref[pl.ds(kstart, rows), :], q_ref[pl.ds(qstart, cb), :])
        if mask_off is not None:
            key = lax.broadcasted_iota(jnp.int32, s.shape, 0)
            qry = lax.broadcasted_iota(jnp.int32, s.shape, 1)
            s = jnp.where(key <= qry + mask_off, s, MASKED)
        m_prev = m_refs[c][...] if qrow is None else jnp.full(m_refs[c].shape, MASKED, F32)
        m_new = jnp.maximum(m_prev, jnp.max(s, axis=0, keepdims=True))
        s_refs[slot][c][:rows, :] = s
        mnew_refs[slot][c][...] = m_new
        alpha_refs[slot][c][...] = jnp.exp2(m_prev - m_new)
        m_refs[c][...] = m_new

    def update(c, kstart, rows, slot):
        alpha = alpha_refs[slot][c][...]
        p = jnp.exp2(s_refs[slot][c][:rows, :] - mnew_refs[slot][c][...])
        l_refs[c][...] = alpha * l_refs[c][...] + jnp.sum(p, axis=0, keepdims=True)
        pv = _dot(vt_ref[:, pl.ds(kstart, rows)], p.astype(BF16))
        acc_refs[c][...] = alpha * acc_refs[c][...] + pv

    def kv(t):
        return pl.multiple_of(t * tk, tk)

    assert tq == 2 * tk

    @pl.when(qi > 0)
    def _():
        def pair(i):
            for c in range(ncb):
                scores(c, kv(2 * i + 1), tk, 1)
                update(c, kv(2 * i), tk, 0)
            for c in range(ncb):
                scores(c, kv(2 * i + 2), tk, 0)
                update(c, kv(2 * i + 1), tk, 1)

        def body(t, carry):
            for u in range(4):
                pair(4 * t + u)
            return carry

        n_pairs = qi - 1
        lax.fori_loop(0, lax.shift_right_logical(n_pairs, 2), body, 0)
        done = n_pairs & ~3

        @pl.when((n_pairs & 2) != 0)
        def _():
            pair(done)
            pair(done + 1)

        @pl.when((n_pairs & 1) != 0)
        def _():
            pair(done + (n_pairs & 2))

        for c in range(ncb):
            scores(c, kv(2 * qi - 1), tk, 1)
            update(c, kv(2 * qi - 2), tk, 0)

    @pl.when(qi == 0)
    def _():
        for c in range(ncb):
            m_refs[c][...] = jnp.full(m_refs[c].shape, MASKED, F32)
            s_refs[1][c][...] = jnp.zeros(s_refs[1][c].shape, F32)
            mnew_refs[1][c][...] = jnp.full(mnew_refs[1][c].shape, -MASKED, F32)
            alpha_refs[1][c][...] = jnp.ones(alpha_refs[1][c].shape, F32)

    pending = kv(jnp.maximum(2 * qi - 1, 0))
    diag = [[], []]
    for d in range(2):
        for c in range(ncb):
            rows = min(tk, (c + 1) * cb - d * tk)
            if rows > 0:
                crosses = d * tk + rows - 1 > c * cb
                diag[d].append(((c, pl.multiple_of(base + d * tk, tk), rows, d),
                                c * cb - d * tk if crosses else None))
    assert [item[0] for item, _ in diag[0]] == list(range(ncb))
    for item, mask_off in diag[0]:
        scores(*item, mask_off)
        update(item[0], pending, tk, 1)
    for n, (item, mask_off) in enumerate(diag[1]):
        scores(*item, mask_off)
        update(*diag[0][n][0])
    rest = [item for item, _ in diag[0][len(diag[1]):] + diag[1]]
    assert len(rest) == ncb and all(rest[c - len(diag[1])][0::3] == (c, 0)
                                    for c in range(len(diag[1]), ncb))
    nxt = pl.multiple_of(jnp.minimum(qi + 1, nq - 1) * tq, tq)
    for c, item in enumerate(rest):
        scores(c, 0, tk, 0, qrow=nxt)
        update(*item)

    for c in range(ncb):
        o = (acc_refs[c][...] / l_refs[c][...]).T
        rs = slice(c * cb, (c + 1) * cb)
        o_ref[rs, :] = (o * g_ref[rs, :].astype(F32)).astype(o_ref.dtype)


def _flash(q, k, vt, proj, zb_off, tq, tk, cb):
    seq = q.shape[0]
    ncb = tq // cb
    stat = [pltpu.VMEM((1, cb), F32)] * ncb
    scratch = (stat + stat + [pltpu.VMEM((V_DIM, cb), F32)] * ncb
               + [pltpu.VMEM((tk, cb), F32)] * (2 * ncb)
               + stat * 4)
    return pl.pallas_call(
        functools.partial(_flash_kernel, tq=tq, tk=tk, cb=cb),
        grid=(MLA_HEADS, seq // tq),
        in_specs=[pl.BlockSpec((seq, HEAD_PAD), lambda h, i: (0, h)),
                  pl.BlockSpec((seq, HEAD_PAD), lambda h, i: (0, h)),
                  pl.BlockSpec((V_DIM, seq), lambda h, i: (h, 0)),
                  pl.BlockSpec((tq, V_DIM), lambda h, i: (i, zb_off // V_DIM + h))],
        out_specs=pl.BlockSpec((tq, V_DIM), lambda h, i: (i, h)),
        out_shape=jax.ShapeDtypeStruct((seq, MLA_HEADS * V_DIM), BF16),
        scratch_shapes=scratch,
        compiler_params=_cparams("arbitrary", "arbitrary"),
        name="flash_mla",
    )(q, k, vt, proj)


def _mem_kv_kernel(mem_ref, g_ref, w_ref, o_ref):
    mn = _rms(mem_ref[...], g_ref[...]).astype(BF16)
    o_ref[...] = _dot(mn, w_ref[...].astype(BF16)).astype(o_ref.dtype)


def _mem_kv(mem2, mem_norm_g, w_mem_kv, tn):
    m, d = mem2.shape
    n = w_mem_kv.shape[1]
    return pl.pallas_call(
        _mem_kv_kernel,
        grid=(n // tn,),
        in_specs=[pl.BlockSpec((m, d), lambda j: (0, 0)),
                  pl.BlockSpec((1, d), lambda j: (0, 0)),
                  pl.BlockSpec((d, tn), lambda j: (0, j))],
        out_specs=pl.BlockSpec((m, tn), lambda j: (0, j)),
        out_shape=jax.ShapeDtypeStruct((m, n), BF16),
        compiler_params=_cparams("parallel"),
        name="mem_kv",
    )(mem2, mem_norm_g, w_mem_kv)


def _mem_attn_kernel(q_ref, kv_ref, sz_ref, o_ref, *, width):
    hd = width // MEM_HEADS
    scale = hd ** -0.5
    for h in range(MEM_HEADS):
        cs = slice(h * hd, (h + 1) * hd)
        s = _dot_nt(q_ref[:, cs], kv_ref[:, cs]) * scale
        p = jnp.exp(s - jnp.max(s, axis=-1, keepdims=True))
        l = jnp.sum(p, axis=-1, keepdims=True)
        o = _dot(p.astype(BF16), kv_ref[:, width + h * hd:width + (h + 1) * hd]) / l
        o_ref[:, cs] = (o * sz_ref[:, cs].astype(F32)).astype(o_ref.dtype)


def _mem_attn(proj, kvm, offs, width, tm):
    seq = proj.shape[0]
    blk = lambda off: pl.BlockSpec((tm, width), lambda i: (i, off // width))
    return pl.pallas_call(
        functools.partial(_mem_attn_kernel, width=width),
        grid=(seq // tm,),
        in_specs=[blk(offs["qm"]), pl.BlockSpec(kvm.shape, lambda i: (0, 0)), blk(offs["zm"])],
        out_specs=pl.BlockSpec((tm, width), lambda i: (i, 0)),
        out_shape=jax.ShapeDtypeStruct((seq, width), BF16),
        compiler_params=_cparams("parallel"),
        name="mem_attn",
    )(proj, kvm, proj)


def _merge_kernel(ya_ref, yb_ref, ym_ref, w_ref, g0_ref, g1_ref, g2_ref, o_ref):
    acc = g0_ref[...].astype(F32) * _dot(ya_ref[...], w_ref[0])
    acc += g1_ref[...].astype(F32) * _dot(yb_ref[...], w_ref[1])
    acc += g2_ref[...].astype(F32) * _dot(ym_ref[...], w_ref[2])
    o_ref[...] = acc.astype(o_ref.dtype)


def _merge(y_a, y_b, y_m, w_branch, proj, gate_off, tm, tn):
    seq, width = y_a.shape
    d = w_branch.shape[2]
    yblk = pl.BlockSpec((tm, width), lambda i, j: (i, 0))
    gblk = lambda n: pl.BlockSpec((tm, tn), lambda i, j, n=n: (i, (gate_off + n * d) // tn + j))
    return pl.pallas_call(
        _merge_kernel,
        grid=(seq // tm, d // tn),
        in_specs=[yblk, yblk, yblk,
                  pl.BlockSpec((N_BRANCH, width, tn), lambda i, j: (0, 0, j)),
                  gblk(0), gblk(1), gblk(2)],
        out_specs=pl.BlockSpec((tm, tn), lambda i, j: (i, j)),
        out_shape=jax.ShapeDtypeStruct((seq, d), BF16),
        compiler_params=_cparams("parallel", "arbitrary"),
        name="merge",
    )(y_a, y_b, y_m, w_branch, proj, proj, proj)


def _out_kernel(m_ref, w_ref, x_ref, g_ref, o_ref):
    out = _dot(m_ref[...], w_ref[...])
    o_ref[...] = x_ref[...] + _rms(out, g_ref[...])


def _out_proj(merged, w_out, x2, g_post, tm):
    seq, d = x2.shape
    return pl.pallas_call(
        _out_kernel,
        grid=(seq // tm,),
        in_specs=[pl.BlockSpec((tm, d), lambda i: (i, 0)),
                  pl.BlockSpec((d, d), lambda i: (0, 0)),
                  pl.BlockSpec((tm, d), lambda i: (i, 0)),
                  pl.BlockSpec((1, d), lambda i: (0, 0))],
        out_specs=pl.BlockSpec((tm, d), lambda i: (i, 0)),
        out_shape=jax.ShapeDtypeStruct((seq, d), F32),
        compiler_params=_cparams("parallel"),
        name="out_proj",
    )(merged, w_out, x2, g_post)


def _layer(x2, mem2, tabs, g_pre, w_in, a_ln_g, a_ln_b, a_w_s, a_b_s, q_norm_g, w_uq,
           kv_norm_g, w_ukv, mem_norm_g, w_mem_kv, w_gate, b_gate, w_branch, w_out, g_post):
    seq, d = x2.shape
    tn = TILES["in_proj"]["tn"]
    w_in_t = jnp.swapaxes(w_in, 0, 1)
    lat, tail = 3 * d, 3 * d + Q_LORA + KV_LORA
    w_kr = jnp.pad(w_in_t[tail:tail + QK_ROPE], ((0, LANE - QK_ROPE), (0, 0))).astype(BF16)
    sections = (("u", _gelu, 0, d), ("v", _gelu, d, d), ("za", _silu, 2 * d, d),
                ("zb", _silu, tail + QK_ROPE, d), ("qm", _identity, tail + QK_ROPE + d, d),
                ("zm", _silu, tail + QK_ROPE + 2 * d, d), ("gate", _sigmoid, None, N_BRANCH * d),
                ("cq", _identity, lat, Q_LORA + KV_LORA))
    offs, tiles, col = {}, [], 0
    for name, act, row, width in sections:
        offs[name] = col
        tiles += [(act, None if row is None else row + t) for t in range(0, width, tn)]
        col += width
    offs["ckv"] = offs["cq"] + Q_LORA

    proj, kr = _projection(x2, g_pre.reshape(1, d), w_in_t, w_gate.astype(BF16),
                           b_gate.reshape(1, -1), w_kr, tuple(tiles), **TILES["in_proj"])

    b_s_b = jnp.broadcast_to(a_b_s[:, :, None], (A_GROUPS, CHUNK, CHUNK))
    y_a = _branch_a(proj, a_ln_g.reshape(1, d), a_ln_b.reshape(1, d), a_w_s, b_s_b, offs,
                    **TILES["branch_a"])

    wq_pad = jnp.pad(w_uq.reshape(Q_LORA, MLA_HEADS, QK_DIM),
                     ((0, 0), (0, 0), (0, HEAD_PAD - QK_DIM))).reshape(Q_LORA, -1).astype(BF16)
    q, k, vv = _mla_prep(proj, kr, tabs, q_norm_g.reshape(1, -1), kv_norm_g.reshape(1, -1),
                         wq_pad, w_ukv.astype(BF16), offs, **TILES["mla_prep"])
    y_b = _flash(q, k, vv, proj, offs["zb"], **TILES["flash"])

    kvm = _mem_kv(mem2, mem_norm_g.reshape(1, d), w_mem_kv, **TILES["mem_kv"])
    y_m = _mem_attn(proj, kvm, offs, d, **TILES["mem_attn"])

    merged = _merge(y_a, y_b, y_m, w_branch.astype(BF16), proj, offs["gate"], **TILES["merge"])
    return _out_proj(merged, w_out.astype(BF16), x2, g_post.reshape(1, d), **TILES["out_proj"])


def kernel(x, mem, positions, g_pre, w_in, a_ln_g, a_ln_b, a_w_s, a_b_s, q_norm_g, w_uq,
           kv_norm_g, w_ukv, mem_norm_g, w_mem_kv, w_gate, b_gate, w_branch, w_out, g_post):
    batch, seq, d = x.shape
    depth = g_pre.shape[0]
    assert d == A_GROUPS * CHUNK == MLA_HEADS * V_DIM
    assert w_in.shape[-1] == 6 * d + Q_LORA + KV_LORA + QK_ROPE
    assert all(seq % t.get("tm", t.get("tq", 1)) == 0 for t in TILES.values())
    outs = []
    for b in range(batch):
        tabs = _rope_tables(positions[b], seq)
        xb = x[b]
        for l in range(depth):
            xb = _layer(xb, mem[b], tabs, g_pre[l], w_in[l], a_ln_g[l], a_ln_b[l], a_w_s[l],
                        a_b_s[l], q_norm_g[l], w_uq[l], kv_norm_g[l], w_ukv[l], mem_norm_g[l],
                        w_mem_kv[l], w_gate[l], b_gate[l], w_branch[l], w_out[l], g_post[l])
        outs.append(xb)
    return jnp.stack(outs, axis=0)
```

```python
import functools

import jax
import jax.numpy as jnp
from jax import lax
from jax.experimental import pallas as pl
from jax.experimental.pallas import tpu as pltpu

EPS = 1e-6
CHUNK = 128
A_GROUPS = 16
QK_NOPE = 128
QK_ROPE = 64
V_DIM = 128
MLA_HEADS = 16
Q_LORA = 512
KV_LORA = 512
QK_DIM = QK_NOPE + QK_ROPE
ROPE_THETA = 10000.0
LOG2E = 1.4426950408889634
GELU_C = 0.7978845608028654
PROJ_CHUNK = 256
MEM_HEADS = 4
N_BRANCH = 3
LANE = 128
HEAD_PAD = 256
VMEM_LIMIT = 52 * 1024 * 1024
PROJ_VMEM_LIMIT = 56 * 1024 * 1024
MASKED = -1e30

TILES = {
    "in_proj": dict(tm=1024, tn=1024),
    "branch_a": dict(tm=512),
    "mla_prep": dict(tm=512),
    "flash": dict(tq=1024, tk=512, cb=PROJ_CHUNK),
    "mem_kv": dict(tn=512),
    "mem_attn": dict(tm=1024),
    "merge": dict(tm=1024, tn=512),
    "out_proj": dict(tm=512),
}

F32 = jnp.float32
BF16 = jnp.bfloat16


def _cparams(*sem):
    return pltpu.CompilerParams(dimension_semantics=sem, vmem_limit_bytes=VMEM_LIMIT)


def _dot(a, b):
    return jnp.dot(a, b, preferred_element_type=F32)


def _dot_nt(a, b):
    return lax.dot_general(a, b, (((1,), (1,)), ((), ())), preferred_element_type=F32)


def _gelu(x):
    half = 0.5 * x
    return half + half * jnp.tanh(x * (GELU_C + (GELU_C * 0.044715) * (x * x)))


def _sigmoid(x):
    return 0.5 + 0.5 * jnp.tanh(0.5 * x)


def _silu(x):
    half = 0.5 * x
    return half + half * jnp.tanh(half)


def _identity(x):
    return x


def _rms(xf, g):
    return xf * lax.rsqrt(jnp.mean(xf * xf, axis=-1, keepdims=True) + EPS) * g


def _rope_table_kernel(pos_ref, invf_ref, tc_ref, ta_ref, tb_ref):
    ang = pos_ref[...].astype(F32) * invf_ref[...]
    cos, sin = jnp.cos(ang), jnp.sin(ang)
    z32 = jnp.zeros_like(cos)
    z64 = jnp.concatenate([z32, z32], axis=0)
    tc_ref[...] = jnp.concatenate([cos, cos, z64], axis=0).T
    ta_ref[...] = jnp.concatenate([sin, z32, z64], axis=0).T
    tb_ref[...] = jnp.concatenate([z32, sin, z64], axis=0).T


def _rope_tables(positions, seq):
    half = QK_ROPE // 2
    inv_freq = 1.0 / (ROPE_THETA ** (jnp.arange(0, QK_ROPE, 2, dtype=F32) / QK_ROPE))
    return pl.pallas_call(
        _rope_table_kernel,
        out_shape=(jax.ShapeDtypeStruct((seq, LANE), F32),) * 3,
        compiler_params=pltpu.CompilerParams(vmem_limit_bytes=VMEM_LIMIT),
        name="rope_tables",
    )(positions.reshape(1, seq), inv_freq.reshape(half, 1))


def _rope(t, tab_c, tab_a, tab_b):
    down = pltpu.roll(t, 3 * LANE // 4, axis=1)
    up = pltpu.roll(t, LANE // 4, axis=1)
    return t * tab_c - down * tab_a + up * tab_b


def _proj_kernel(x_ref, g_ref, wt_ref, wg_ref, b_ref, wkr_ref, o_ref, kr_ref, h_ref, *, tiles):
    j = pl.program_id(1)

    def tile(act, is_gate):
        for lo in range(0, o_ref.shape[1], PROJ_CHUNK):
            cs = slice(lo, lo + PROJ_CHUNK)
            if is_gate:
                acc = _dot(h_ref[...], wg_ref[:, cs]) + b_ref[:, cs]
            else:
                acc = _dot_nt(h_ref[...], wt_ref[cs, :].astype(BF16))
            o_ref[:, cs] = act(acc).astype(o_ref.dtype)

    kind = lambda t: (tiles[t][0], tiles[t][1] is None)

    @pl.when(j == 0)
    def _():
        h_ref[...] = _rms(x_ref[...], g_ref[...]).astype(BF16)
        tile(*kind(0))

    later = range(1, len(tiles))
    for k in sorted({kind(t) for t in later}, key=lambda k: (k[0].__name__, k[1])):
        @pl.when(functools.reduce(jnp.logical_or, [j == t for t in later if kind(t) == k]))
        def _(k=k):
            tile(*k)

    @pl.when(j == len(tiles) - 1)
    def _():
        kr_ref[...] = _dot_nt(h_ref[...], wkr_ref[...]).astype(kr_ref.dtype)


def _projection(x2, g_pre, w_in_t, w_gate, b_gate, w_kr, tiles, tm, tn):
    seq, d = x2.shape
    n_tiles = len(tiles)

    def pick(j, values):
        return functools.reduce(lambda acc, tv: jnp.where(j == tv[0], tv[1], acc),
                                enumerate(values), values[0])

    rows, gates, row, gate = [], [], 0, 0
    for act, src in tiles:
        if src is None:
            gates.append(gate)
            gate += 1
        else:
            row = src
            gates.append(max(gate - 1, 0))
        rows.append(row)
    return pl.pallas_call(
        functools.partial(_proj_kernel, tiles=tiles),
        grid=(seq // tm, n_tiles),
        in_specs=[
            pl.BlockSpec((tm, d), lambda i, j: (i, 0)),
            pl.BlockSpec((1, d), lambda i, j: (0, 0)),
            pl.BlockSpec((pl.Element(tn), pl.Element(d)),
                         lambda i, j: (pick(j, [r // QK_ROPE for r in rows]) * QK_ROPE, 0)),
            pl.BlockSpec((d, tn), lambda i, j: (0, pick(j, gates))),
            pl.BlockSpec((1, tn), lambda i, j: (0, pick(j, gates))),
            pl.BlockSpec((LANE, d), lambda i, j: (0, 0)),
        ],
        out_specs=[pl.BlockSpec((tm, tn), lambda i, j: (i, j)),
                   pl.BlockSpec((tm, LANE), lambda i, j: (i, 0))],
        out_shape=[jax.ShapeDtypeStruct((seq, n_tiles * tn), BF16),
                   jax.ShapeDtypeStruct((seq, LANE), BF16)],
        scratch_shapes=[pltpu.VMEM((tm, d), BF16)],
        compiler_params=pltpu.CompilerParams(dimension_semantics=("parallel", "arbitrary"),
                                             vmem_limit_bytes=PROJ_VMEM_LIMIT),
        name="in_proj",
    )(x2, g_pre, w_in_t, w_gate, b_gate, w_kr)


def _branch_a_kernel(gu_ref, gv_ref, sz_ref, lng_ref, lnb_ref, ws_ref, bs_ref, o_ref, *, tm):
    gv = gv_ref[...].astype(F32)
    mu = jnp.mean(gv, axis=-1, keepdims=True)
    cen = gv - mu
    var = jnp.mean(cen * cen, axis=-1, keepdims=True)
    vn = (cen * lax.rsqrt(var + EPS) * lng_ref[...] + lnb_ref[...]).astype(BF16)
    row = lax.broadcasted_iota(jnp.int32, (CHUNK, CHUNK), 0)
    col = lax.broadcasted_iota(jnp.int32, (CHUNK, CHUNK), 1)
    causal = col <= row
    for g in range(A_GROUPS):
        ws = jnp.where(causal, ws_ref[g], 0.0).astype(BF16)
        cs = slice(g * CHUNK, (g + 1) * CHUNK)
        for c in range(tm // CHUNK):
            rs = slice(c * CHUNK, (c + 1) * CHUNK)
            sv = _dot(ws, vn[rs, cs]) + bs_ref[g]
            y = gu_ref[rs, cs].astype(F32) * sv * sz_ref[rs, cs].astype(F32)
            o_ref[rs, cs] = y.astype(o_ref.dtype)


def _branch_a(proj, ln_g, ln_b, w_s, b_s_b, offs, tm):
    seq = proj.shape[0]
    w = A_GROUPS * CHUNK
    blk = lambda off: pl.BlockSpec((tm, w), lambda i, off=off: (i, off // w))
    const = lambda shape: pl.BlockSpec(shape, lambda i: (0,) * len(shape))
    return pl.pallas_call(
        functools.partial(_branch_a_kernel, tm=tm),
        grid=(seq // tm,),
        in_specs=[blk(offs["u"]), blk(offs["v"]), blk(offs["za"]),
                  const((1, w)), const((1, w)),
                  const((A_GROUPS, CHUNK, CHUNK)), const((A_GROUPS, CHUNK, CHUNK))],
        out_specs=pl.BlockSpec((tm, w), lambda i: (i, 0)),
        out_shape=jax.ShapeDtypeStruct((seq, w), BF16),
        compiler_params=_cparams("parallel"),
        name="branch_a",
    )(proj, proj, proj, ln_g, ln_b, w_s, b_s_b)


def _mla_prep_kernel(cq_ref, ckv_ref, kr_ref, tc_ref, ta_ref, tb_ref, qg_ref, kvg_ref,
                     wq_ref, wkv_ref, q_ref, k_ref, vt_ref):
    scale = QK_DIM ** -0.5 * LOG2E
    tab_c, tab_a, tab_b = tc_ref[...], ta_ref[...], tb_ref[...]
    cqn = _rms(cq_ref[...].astype(F32), qg_ref[...]).astype(BF16)
    ckvn = _rms(ckv_ref[...].astype(F32), kvg_ref[...]).astype(BF16)
    kpe = _rope(kr_ref[...].astype(F32), tab_c, tab_a, tab_b).astype(BF16)
    for h in range(MLA_HEADS):
        lo, mid, hi = h * HEAD_PAD, h * HEAD_PAD + LANE, (h + 1) * HEAD_PAD
        q = _dot(cqn, wq_ref[:, lo:hi]) * scale
        q_ref[:, lo:mid] = q[:, :LANE].astype(BF16)
        q_ref[:, mid:hi] = _rope(q[:, LANE:], tab_c, tab_a, tab_b).astype(BF16)
        kv = _dot(ckvn, wkv_ref[:, lo:hi].astype(BF16))
        k_ref[:, lo:mid] = kv[:, :LANE].astype(BF16)
        k_ref[:, mid:hi] = kpe
        vt_ref[h * V_DIM:(h + 1) * V_DIM, :] = kv[:, LANE:].T.astype(BF16)


def _mla_prep(proj, kr, tabs, q_norm_g, kv_norm_g, wq_pad, wkv, offs, tm):
    seq = proj.shape[0]
    hw = MLA_HEADS * HEAD_PAD
    const = lambda shape: pl.BlockSpec(shape, lambda i: (0,) * len(shape))
    col = lambda off, w: pl.BlockSpec((tm, w), lambda i: (i, off // w))
    tab = pl.BlockSpec((tm, LANE), lambda i: (i, 0))
    return pl.pallas_call(
        _mla_prep_kernel,
        grid=(seq // tm,),
        in_specs=[col(offs["cq"], Q_LORA), col(offs["ckv"], KV_LORA),
                  tab, tab, tab, tab, const((1, Q_LORA)), const((1, KV_LORA)),
                  const((Q_LORA, hw)), const((KV_LORA, hw))],
        out_specs=[pl.BlockSpec((tm, hw), lambda i: (i, 0)),
                   pl.BlockSpec((tm, hw), lambda i: (i, 0)),
                   pl.BlockSpec((MLA_HEADS * V_DIM, tm), lambda i: (0, i))],
        out_shape=[jax.ShapeDtypeStruct((seq, hw), BF16),
                   jax.ShapeDtypeStruct((seq, hw), BF16),
                   jax.ShapeDtypeStruct((MLA_HEADS * V_DIM, seq), BF16)],
        compiler_params=_cparams("parallel"),
        name="mla_prep",
    )(proj, proj, kr, *tabs, q_norm_g, kv_norm_g, wq_pad, wkv)


def _flash_kernel(q_ref, k_ref, vt_ref, g_ref, o_ref, *scratch, tq, tk, cb):
    qi = pl.program_id(1)
    nq = q_ref.shape[0] // tq
    ncb = tq // cb
    base = pl.multiple_of(qi * tq, tq)
    take = lambda n: [scratch[i * ncb:(i + 1) * ncb] for i in range(n)]
    m_refs, l_refs, acc_refs, s0, s1, mn0, mn1, al0, al1 = take(9)
    s_refs, mnew_refs, alpha_refs = (s0, s1), (mn0, mn1), (al0, al1)
    for c in range(ncb):
        l_refs[c][...] = jnp.zeros(l_refs[c].shape, F32)
        acc_refs[c][...] = jnp.zeros(acc_refs[c].shape, F32)

    def scores(c, kstart, rows, slot, mask_off=None, qrow=None):
        qstart = pl.multiple_of((base if qrow is None else qrow) + c * cb, cb)
        s = _dot_nt(k_ref[pl.ds(kstart, rows), :], q_ref[pl.ds(qstart, cb), :])
        if mask_off is not None:
            key = lax.broadcasted_iota(jnp.int32, s.shape, 0)
            qry = lax.broadcasted_iota(jnp.int32, s.shape, 1)
            s = jnp.where(key <= qry + mask_off, s, MASKED)
        m_prev = m_refs[c][...] if qrow is None else jnp.full(m_refs[c].shape, MASKED, F32)
        m_new = jnp.maximum(m_prev, jnp.max(s, axis=0, keepdims=True))
        s_refs[slot][c][:rows, :] = s
        mnew_refs[slot][c][...] = m_new
        alpha_refs[slot][c][...] = jnp.exp2(m_prev - m_new)
        m_refs[c][...] = m_new

    def update(c, kstart, rows, slot):
        alpha = alpha_refs[slot][c][...]
        p = jnp.exp2(s_refs[slot][c][:rows, :] - mnew_refs[slot][c][...])
        l_refs[c][...] = alpha * l_refs[c][...] + jnp.sum(p, axis=0, keepdims=True)
        pv = _dot(vt_ref[:, pl.ds(kstart, rows)], p.astype(BF16))
        acc_refs[c][...] = alpha * acc_refs[c][...] + pv

    def kv(t):
        return pl.multiple_of(t * tk, tk)

    assert tq == 2 * tk

    @pl.when(qi > 0)
    def _():
        def pair(i):
            for c in range(ncb):
                scores(c, kv(2 * i + 1), tk, 1)
                update(c, kv(2 * i), tk, 0)
            for c in range(ncb):
                scores(c, kv(2 * i + 2), tk, 0)
                update(c, kv(2 * i + 1), tk, 1)

        def body(t, carry):
            for u in range(4):
                pair(4 * t + u)
            return carry

        n_pairs = qi - 1
        lax.fori_loop(0, lax.shift_right_logical(n_pairs, 2), body, 0)
        done = n_pairs & ~3

        @pl.when((n_pairs & 2) != 0)
        def _():
            pair(done)
            pair(done + 1)

        @pl.when((n_pairs & 1) != 0)
        def _():
            pair(done + (n_pairs & 2))

        for c in range(ncb):
            scores(c, kv(2 * qi - 1), tk, 1)
            update(c, kv(2 * qi - 2), tk, 0)

    @pl.when(qi == 0)
    def _():
        for c in range(ncb):
            m_refs[c][...] = jnp.full(m_refs[c].shape, MASKED, F32)
            s_refs[1][c][...] = jnp.zeros(s_refs[1][c].shape, F32)
            mnew_refs[1][c][...] = jnp.full(mnew_refs[1][c].shape, -MASKED, F32)
            alpha_refs[1][c][...] = jnp.ones(alpha_refs[1][c].shape, F32)

    pending = kv(jnp.maximum(2 * qi - 1, 0))
    diag = [[], []]
    for d in range(2):
        for c in range(ncb):
            rows = min(tk, (c + 1) * cb - d * tk)
            if rows > 0:
                crosses = d * tk + rows - 1 > c * cb
                diag[d].append(((c, pl.multiple_of(base + d * tk, tk), rows, d),
                                c * cb - d * tk if crosses else None))
    assert [item[0] for item, _ in diag[0]] == list(range(ncb))
    for item, mask_off in diag[0]:
        scores(*item, mask_off)
        update(item[0], pending, tk, 1)
    for n, (item, mask_off) in enumerate(diag[1]):
        scores(*item, mask_off)
        update(*diag[0][n][0])
    rest = [item for item, _ in diag[0][len(diag[1]):] + diag[1]]
    assert len(rest) == ncb and all(rest[c - len(diag[1])][0::3] == (c, 0)
                                    for c in range(len(diag[1]), ncb))
    nxt = pl.multiple_of(jnp.minimum(qi + 1, nq - 1) * tq, tq)
    for c, item in enumerate(rest):
        scores(c, 0, tk, 0, qrow=nxt)
        update(*item)

    for c in range(ncb):
        o = (acc_refs[c][...] / l_refs[c][...]).T
        rs = slice(c * cb, (c + 1) * cb)
        o_ref[rs, :] = (o * g_ref[rs, :].astype(F32)).astype(o_ref.dtype)


def _flash(q, k, vt, proj, zb_off, tq, tk, cb):
    seq = q.shape[0]
    ncb = tq // cb
    stat = [pltpu.VMEM((1, cb), F32)] * ncb
    scratch = (stat + stat + [pltpu.VMEM((V_DIM, cb), F32)] * ncb
               + [pltpu.VMEM((tk, cb), F32)] * (2 * ncb)
               + stat * 4)
    return pl.pallas_call(
        functools.partial(_flash_kernel, tq=tq, tk=tk, cb=cb),
        grid=(MLA_HEADS, seq // tq),
        in_specs=[pl.BlockSpec((seq, HEAD_PAD), lambda h, i: (0, h)),
                  pl.BlockSpec((seq, HEAD_PAD), lambda h, i: (0, h)),
                  pl.BlockSpec((V_DIM, seq), lambda h, i: (h, 0)),
                  pl.BlockSpec((tq, V_DIM), lambda h, i: (i, zb_off // V_DIM + h))],
        out_specs=pl.BlockSpec((tq, V_DIM), lambda h, i: (i, h)),
        out_shape=jax.ShapeDtypeStruct((seq, MLA_HEADS * V_DIM), BF16),
        scratch_shapes=scratch,
        compiler_params=_cparams("arbitrary", "arbitrary"),
        name="flash_mla",
    )(q, k, vt, proj)


def _mem_kv_kernel(mem_ref, g_ref, w_ref, o_ref):
    mn = _rms(mem_ref[...], g_ref[...]).astype(BF16)
    o_ref[...] = _dot(mn, w_ref[...].astype(BF16)).astype(o_ref.dtype)


def _mem_kv(mem2, mem_norm_g, w_mem_kv, tn):
    m, d = mem2.shape
    n = w_mem_kv.shape[1]
    return pl.pallas_call(
        _mem_kv_kernel,
        grid=(n // tn,),
        in_specs=[pl.BlockSpec((m, d), lambda j: (0, 0)),
                  pl.BlockSpec((1, d), lambda j: (0, 0)),
                  pl.BlockSpec((d, tn), lambda j: (0, j))],
        out_specs=pl.BlockSpec((m, tn), lambda j: (0, j)),
        out_shape=jax.ShapeDtypeStruct((m, n), BF16),
        compiler_params=_cparams("parallel"),
        name="mem_kv",
    )(mem2, mem_norm_g, w_mem_kv)


def _mem_attn_kernel(q_ref, kv_ref, sz_ref, o_ref, *, width):
    hd = width // MEM_HEADS
    scale = hd ** -0.5
    for h in range(MEM_HEADS):
        cs = slice(h * hd, (h + 1) * hd)
        s = _dot_nt(q_ref[:, cs], kv_ref[:, cs]) * scale
        p = jnp.exp(s - jnp.max(s, axis=-1, keepdims=True))
        l = jnp.sum(p, axis=-1, keepdims=True)
        o = _dot(p.astype(BF16), kv_ref[:, width + h * hd:width + (h + 1) * hd]) / l
        o_ref[:, cs] = (o * sz_ref[:, cs].astype(F32)).astype(o_ref.dtype)


def _mem_attn(proj, kvm, offs, width, tm):
    seq = proj.shape[0]
    blk = lambda off: pl.BlockSpec((tm, width), lambda i: (i, off // width))
    return pl.pallas_call(
        functools.partial(_mem_attn_kernel, width=width),
        grid=(seq // tm,),
        in_specs=[blk(offs["qm"]), pl.BlockSpec(kvm.shape, lambda i: (0, 0)), blk(offs["zm"])],
        out_specs=pl.BlockSpec((tm, width), lambda i: (i, 0)),
        out_shape=jax.ShapeDtypeStruct((seq, width), BF16),
        compiler_params=_cparams("parallel"),
        name="mem_attn",
    )(proj, kvm, proj)


def _merge_kernel(ya_ref, yb_ref, ym_ref, w_ref, g0_ref, g1_ref, g2_ref, o_ref):
    acc = g0_ref[...].astype(F32) * _dot(ya_ref[...], w_ref[0])
    acc += g1_ref[...].astype(F32) * _dot(yb_ref[...], w_ref[1])
    acc += g2_ref[...].astype(F32) * _dot(ym_ref[...], w_ref[2])
    o_ref[...] = acc.astype(o_ref.dtype)


def _merge(y_a, y_b, y_m, w_branch, proj, gate_off, tm, tn):
    seq, width = y_a.shape
    d = w_branch.shape[2]
    yblk = pl.BlockSpec((tm, width), lambda i, j: (i, 0))
    gblk = lambda n: pl.BlockSpec((tm, tn), lambda i, j, n=n: (i, (gate_off + n * d) // tn + j))
    return pl.pallas_call(
        _merge_kernel,
        grid=(seq // tm, d // tn),
        in_specs=[yblk, yblk, yblk,
                  pl.BlockSpec((N_BRANCH, width, tn), lambda i, j: (0, 0, j)),
                  gblk(0), gblk(1), gblk(2)],
        out_specs=pl.BlockSpec((tm, tn), lambda i, j: (i, j)),
        out_shape=jax.ShapeDtypeStruct((seq, d), BF16),
        compiler_params=_cparams("parallel", "arbitrary"),
        name="merge",
    )(y_a, y_b, y_m, w_branch, proj, proj, proj)


def _out_kernel(m_ref, w_ref, x_ref, g_ref, o_ref):
    out = _dot(m_ref[...], w_ref[...])
    o_ref[...] = x_ref[...] + _rms(out, g_ref[...])


def _out_proj(merged, w_out, x2, g_post, tm):
    seq, d = x2.shape
    return pl.pallas_call(
        _out_kernel,
        grid=(seq // tm,),
        in_specs=[pl.BlockSpec((tm, d), lambda i: (i, 0)),
                  pl.BlockSpec((d, d), lambda i: (0, 0)),
                  pl.BlockSpec((tm, d), lambda i: (i, 0)),
                  pl.BlockSpec((1, d), lambda i: (0, 0))],
        out_specs=pl.BlockSpec((tm, d), lambda i: (i, 0)),
        out_shape=jax.ShapeDtypeStruct((seq, d), F32),
        compiler_params=_cparams("parallel"),
        name="out_proj",
    )(merged, w_out, x2, g_post)


def _layer(x2, mem2, tabs, g_pre, w_in, a_ln_g, a_ln_b, a_w_s, a_b_s, q_norm_g, w_uq,
           kv_norm_g, w_ukv, mem_norm_g, w_mem_kv, w_gate, b_gate, w_branch, w_out, g_post):
    seq, d = x2.shape
    tn = TILES["in_proj"]["tn"]
    w_in_t = jnp.swapaxes(w_in, 0, 1)
    lat, tail = 3 * d, 3 * d + Q_LORA + KV_LORA
    w_kr = jnp.pad(w_in_t[tail:tail + QK_ROPE], ((0, LANE - QK_ROPE), (0, 0))).astype(BF16)
    sections = (("u", _gelu, 0, d), ("v", _gelu, d, d), ("za", _silu, 2 * d, d),
                ("zb", _silu, tail + QK_ROPE, d), ("qm", _identity, tail + QK_ROPE + d, d),
                ("zm", _silu, tail + QK_ROPE + 2 * d, d), ("gate", _sigmoid, None, N_BRANCH * d),
                ("cq", _identity, lat, Q_LORA + KV_LORA))
    offs, tiles, col = {}, [], 0
    for name, act, row, width in sections:
        offs[name] = col
        tiles += [(act, None if row is None else row + t) for t in range(0, width, tn)]
        col += width
    offs["ckv"] = offs["cq"] + Q_LORA

    proj, kr = _projection(x2, g_pre.reshape(1, d), w_in_t, w_gate.astype(BF16),
                           b_gate.reshape(1, -1), w_kr, tuple(tiles), **TILES["in_proj"])

    b_s_b = jnp.broadcast_to(a_b_s[:, :, None], (A_GROUPS, CHUNK, CHUNK))
    y_a = _branch_a(proj, a_ln_g.reshape(1, d), a_ln_b.reshape(1, d), a_w_s, b_s_b, offs,
                    **TILES["branch_a"])

    wq_pad = jnp.pad(w_uq.reshape(Q_LORA, MLA_HEADS, QK_DIM),
                     ((0, 0), (0, 0), (0, HEAD_PAD - QK_DIM))).reshape(Q_LORA, -1).astype(BF16)
    q, k, vv = _mla_prep(proj, kr, tabs, q_norm_g.reshape(1, -1), kv_norm_g.reshape(1, -1),
                         wq_pad, w_ukv, offs, **TILES["mla_prep"])
    y_b = _flash(q, k, vv, proj, offs["zb"], **TILES["flash"])

    kvm = _mem_kv(mem2, mem_norm_g.reshape(1, d), w_mem_kv, **TILES["mem_kv"])
    y_m = _mem_attn(proj, kvm, offs, d, **TILES["mem_attn"])

    merged = _merge(y_a, y_b, y_m, w_branch.astype(BF16), proj, offs["gate"], **TILES["merge"])
    return _out_proj(merged, w_out.astype(BF16), x2, g_post.reshape(1, d), **TILES["out_proj"])


def kernel(x, mem, positions, g_pre, w_in, a_ln_g, a_ln_b, a_w_s, a_b_s, q_norm_g, w_uq,
           kv_norm_g, w_ukv, mem_norm_g, w_mem_kv, w_gate, b_gate, w_branch, w_out, g_post):
    batch, seq, d = x.shape
    depth = g_pre.shape[0]
    assert d == A_GROUPS * CHUNK == MLA_HEADS * V_DIM
    assert w_in.shape[-1] == 6 * d + Q_LORA + KV_LORA + QK_ROPE
    assert all(seq % t.get("tm", t.get("tq", 1)) == 0 for t in TILES.values())
    outs = []
    for b in range(batch):
        tabs = _rope_tables(positions[b], seq)
        xb = x[b]
        for l in range(depth):
            xb = _layer(xb, mem[b], tabs, g_pre[l], w_in[l], a_ln_g[l], a_ln_b[l], a_w_s[l],
                        a_b_s[l], q_norm_g[l], w_uq[l], kv_norm_g[l], w_ukv[l], mem_norm_g[l],
                        w_mem_kv[l], w_gate[l], b_gate[l], w_branch[l], w_out[l], g_post[l])
        outs.append(xb)
    return jnp.stack(outs, axis=0)
```
